```python
import jax, jax.numpy as jnp
from jax import lax
import numpy as np

D_MODEL = 1024
BATCH = 8
SEQ = 2048
DEPTH = 4

HEAD_DIM = 64
N_HEADS = D_MODEL // HEAD_DIM
INNER = N_HEADS * HEAD_DIM
ROT_DIM = HEAD_DIM // 4
ROPE_THETA = 500000.0
DSWA_GROUPS = ((128, 1), (512, 4), (2048, 16))
N_GROUPS = len(DSWA_GROUPS)
BLOCK = 128
N_MIXERS = 2
EPS = 1e-6
FOX_IN = 4 * INNER + N_HEADS
DSW_IN = 3 * N_GROUPS * INNER + INNER

kernel_name = "fox_dilated_swa_gated_hybrid"


def rms_norm(x, g):
    xf = x.astype(jnp.float32)
    y = xf * lax.rsqrt(jnp.mean(xf * xf, axis=-1, keepdims=True) + EPS) * g.astype(jnp.float32)
    return y.astype(x.dtype)


def rotary_tables(positions):
    inv_freq = ROPE_THETA ** (-jnp.arange(0, ROT_DIM, 2, dtype=jnp.float32) / ROT_DIM)
    ang = positions.astype(jnp.float32)[..., None] * inv_freq
    return jnp.cos(ang), jnp.sin(ang)


def apply_partial_rotary(t, cos, sin):
    half = ROT_DIM // 2
    c = cos[:, :, None, None, :].astype(t.dtype)
    s = sin[:, :, None, None, :].astype(t.dtype)
    t1, t2, rest = t[..., :half], t[..., half:ROT_DIM], t[..., ROT_DIM:]
    return jnp.concatenate([t1 * c - t2 * s, t2 * c + t1 * s, rest], axis=-1)


def fox_attention(q, k, v, c):
    B, S, H, dh = q.shape
    nq = S // BLOCK
    scale = dh ** -0.5
    qb = q.reshape(B, nq, BLOCK, H, dh).transpose(1, 0, 2, 3, 4)
    cb = c.reshape(B, nq, BLOCK, H).transpose(1, 0, 3, 2)
    c_k = c.transpose(0, 2, 1)
    key_pos = jnp.arange(S)

    def one_block(args):
        qi, ci, bi = args
        s = jnp.einsum('bqhd,bkhd->bhqk', qi, k).astype(jnp.float32) * scale
        s = s + ci[..., None] - c_k[:, :, None, :]
        q_pos = bi * BLOCK + jnp.arange(BLOCK)
        mask = key_pos[None, :] <= q_pos[:, None]
        s = jnp.where(mask, s, -jnp.inf)
        p = jax.nn.softmax(s, axis=-1).astype(v.dtype)
        return jnp.einsum('bhqk,bkhd->bqhd', p, v)

    o = lax.map(one_block, (qb, cb, jnp.arange(nq)))
    return o.transpose(1, 0, 2, 3, 4).reshape(B, S, H, dh)


def fox_mixer(h, w_in, b_f, q_g, k_g):
    B, S, _ = h.shape
    proj = h @ w_in
    q = rms_norm(proj[..., :INNER].reshape(B, S, N_HEADS, HEAD_DIM), q_g)
    k = rms_norm(proj[..., INNER:2 * INNER].reshape(B, S, N_HEADS, HEAD_DIM), k_g)
    v = proj[..., 2 * INNER:3 * INNER].reshape(B, S, N_HEADS, HEAD_DIM)
    gate = proj[..., 3 * INNER:4 * INNER]
    f_logit = proj[..., 4 * INNER:].astype(jnp.float32) + b_f.astype(jnp.float32)
    c = jnp.cumsum(jax.nn.log_sigmoid(f_logit), axis=1)
    o = fox_attention(q, k, v, c)
    return o.reshape(B, S, INNER), gate


def _band(t):
    prev = jnp.concatenate([jnp.zeros_like(t[:, :1]), t[:, :-1]], axis=1)
    return jnp.concatenate([prev, t], axis=2)


def dilated_window_attention(q, k, v, window, dilation):
    B, S, H, dh = q.shape
    r = dilation
    L = S // r
    n_back = window // r
    nb = -(-L // BLOCK)
    Lp = nb * BLOCK
    scale = dh ** -0.5

    def to_blocks(t):
        t = t.reshape(B, L, r, H, dh).transpose(0, 2, 1, 3, 4).reshape(B * r, L, H, dh)
        t = jnp.pad(t, ((0, 0), (0, Lp - L), (0, 0), (0, 0)))
        return t.reshape(B * r, nb, BLOCK, H, dh)

    qb, kb, vb = to_blocks(q), to_blocks(k), to_blocks(v)
    s = jnp.einsum('znqhd,znkhd->znhqk', qb, _band(kb)).astype(jnp.float32) * scale
    q_idx = jnp.arange(nb)[:, None] * BLOCK + jnp.arange(BLOCK)[None, :]
    k_idx = jnp.arange(nb)[:, None] * BLOCK - BLOCK + jnp.arange(2 * BLOCK)[None, :]
    dist = q_idx[:, :, None] - k_idx[:, None, :]
    mask = (dist >= 0) & (dist <= n_back) & (k_idx[:, None, :] >= 0)
    s = jnp.where(mask[None, :, None], s, -jnp.inf)
    lse = jax.nn.logsumexp(s, axis=-1)
    p = jnp.exp(s - lse[..., None]).astype(v.dtype)
    o = jnp.einsum('znhqk,znkhd->znqhd', p, _band(vb))
    o = o.reshape(B * r, Lp, H, dh)[:, :L]
    o = o.reshape(B, r, L, H, dh).transpose(0, 2, 1, 3, 4).reshape(B, S, H, dh)
    lse = lse.transpose(0, 1, 3, 2).reshape(B * r, Lp, H)[:, :L]
    lse = lse.reshape(B, r, L, H).transpose(0, 2, 1, 3).reshape(B, S, H)
    return o, lse


def dsw_mixer(h, cos, sin, w_in, q_g, k_g):
    B, S, _ = h.shape
    proj = h @ w_in
    qkv = proj[..., :3 * N_GROUPS * INNER].reshape(B, S, 3, N_GROUPS, N_HEADS, HEAD_DIM)
    gate = proj[..., 3 * N_GROUPS * INNER:]
    q = apply_partial_rotary(rms_norm(qkv[:, :, 0], q_g[:, None, :]), cos, sin)
    k = apply_partial_rotary(rms_norm(qkv[:, :, 1], k_g[:, None, :]), cos, sin)
    v = qkv[:, :, 2]
    outs, lses = [], []
    for g, (window, dilation) in enumerate(DSWA_GROUPS):
        o_g, lse_g = dilated_window_attention(q[:, :, g], k[:, :, g], v[:, :, g], window, dilation)
        outs.append(o_g)
        lses.append(lse_g)
    wts = jax.nn.softmax(jnp.stack(lses, axis=0), axis=0).astype(v.dtype)
    o = jnp.einsum('gbsh,gbshd->bshd', wts, jnp.stack(outs, axis=0))
    return o.reshape(B, S, INNER), gate


def setup_inputs(seed: int = 0) -> dict:
    key = jax.random.key(seed)
    ks = jax.random.split(key, 13)
    n_fox = (DEPTH + 1) // N_MIXERS
    n_dsw = DEPTH // N_MIXERS
    f32 = jnp.float32
    x = jax.random.normal(ks[0], (BATCH, SEQ, D_MODEL), f32)
    offset = jax.random.randint(ks[1], (BATCH, 1), 0, 4096, dtype=jnp.int32)
    positions = offset + jnp.arange(SEQ, dtype=jnp.int32)[None, :]
    norm_g = 1.0 + 0.02 * jax.random.normal(ks[2], (DEPTH, D_MODEL), f32)
    fox_w_in = jax.random.normal(ks[3], (n_fox, D_MODEL, FOX_IN), f32) * D_MODEL ** -0.5
    fox_b_f = jax.random.uniform(ks[4], (n_fox, N_HEADS), f32, minval=1.0, maxval=4.0)
    fox_q_norm = 1.0 + 0.02 * jax.random.normal(ks[5], (n_fox, HEAD_DIM), f32)
    fox_k_norm = 1.0 + 0.02 * jax.random.normal(ks[6], (n_fox, HEAD_DIM), f32)
    fox_w_out = jax.random.normal(ks[7], (n_fox, INNER, D_MODEL), f32) * (2 * INNER) ** -0.5
    dsw_w_in = jax.random.normal(ks[8], (n_dsw, D_MODEL, DSW_IN), f32) * D_MODEL ** -0.5
    dsw_q_norm = 1.0 + 0.02 * jax.random.normal(ks[9], (n_dsw, N_GROUPS, HEAD_DIM), f32)
    dsw_k_norm = 1.0 + 0.02 * jax.random.normal(ks[10], (n_dsw, N_GROUPS, HEAD_DIM), f32)
    dsw_w_out = jax.random.normal(ks[11], (n_dsw, INNER, D_MODEL), f32) * (2 * INNER) ** -0.5
    return {"x": x, "positions": positions, "norm_g": norm_g,
            "fox_w_in": fox_w_in, "fox_b_f": fox_b_f, "fox_q_norm": fox_q_norm,
            "fox_k_norm": fox_k_norm, "fox_w_out": fox_w_out,
            "dsw_w_in": dsw_w_in, "dsw_q_norm": dsw_q_norm, "dsw_k_norm": dsw_k_norm,
            "dsw_w_out": dsw_w_out}


def reference(x, positions, norm_g, fox_w_in, fox_b_f, fox_q_norm, fox_k_norm, fox_w_out,
              dsw_w_in, dsw_q_norm, dsw_k_norm, dsw_w_out):
    cos, sin = rotary_tables(positions)
    for i in range(DEPTH):
        j = i // N_MIXERS
        h = rms_norm(x, norm_g[i])
        if i % N_MIXERS == 0:
            o, gate = fox_mixer(h, fox_w_in[j], fox_b_f[j], fox_q_norm[j], fox_k_norm[j])
            w_out = fox_w_out[j]
        else:
            o, gate = dsw_mixer(h, cos, sin, dsw_w_in[j], dsw_q_norm[j], dsw_k_norm[j])
            w_out = dsw_w_out[j]
        x = x + (o * jax.nn.silu(gate)) @ w_out
    return x
```

```python
import functools

import jax
import jax.numpy as jnp
from jax import lax
from jax.experimental import pallas as pl
from jax.experimental.pallas import tpu as pltpu

D_MODEL = 1024
HEAD_DIM = 64
N_HEADS = 16
INNER = N_HEADS * HEAD_DIM
ROT_DIM = HEAD_DIM // 4
ROPE_THETA = 500000.0
DSWA_GROUPS = ((128, 1), (512, 4), (2048, 16))
BLOCK = 128
EPS = 1e-6
SCALE = HEAD_DIM ** -0.5

LANES = 128
HEADS_PER_TILE = LANES // HEAD_DIM
N_TILES = INNER // LANES
NEG = -1e30
VMEM_LIMIT = 48 * 1024 * 1024

ROW_TILE = 512
FOX_TQ = 256
FOX_TK = 256

F32 = jnp.float32
BF16 = jnp.bfloat16


def _lane_is_low(shape):
    return lax.broadcasted_iota(jnp.int32, shape, len(shape) - 1) < HEAD_DIM


def _inproj_kernel(*refs, k, tl, kinds, rotary, with_f):
    it = iter(refs)
    x_ref, ng_ref, w_ref, gain_ref = next(it), next(it), next(it), next(it)
    if rotary:
        c_ref, s1_ref, s2_ref = next(it), next(it), next(it)
    if with_f:
        wf_ref = next(it)
    out_ref = next(it)
    if with_f:
        ft_ref = next(it)
    h_scr = next(it)

    j = pl.program_id(3)

    @pl.when(j == 0)
    def _():
        for i in range(k):
            xi = x_ref[0, :, i * D_MODEL:(i + 1) * D_MODEL]
            ms = jnp.mean(xi * xi, axis=-1, keepdims=True)
            h = xi * lax.rsqrt(ms + EPS) * ng_ref[...]
            h_scr[i * tl:(i + 1) * tl, :] = h.astype(BF16)
        if with_f:
            f = jnp.dot(h_scr[...], wf_ref[...], preferred_element_type=F32)
            ft_ref[0] = f.T

    def head_norm(a, gain):
        low = _lane_is_low(a.shape)
        y = a * a
        s_lo = jnp.sum(jnp.where(low, y, 0.0), axis=-1, keepdims=True)
        s_hi = jnp.sum(jnp.where(low, 0.0, y), axis=-1, keepdims=True)
        r_lo = lax.rsqrt(s_lo * (1.0 / HEAD_DIM) + EPS)
        r_hi = lax.rsqrt(s_hi * (1.0 / HEAD_DIM) + EPS)
        return a * jnp.where(low, r_lo, r_hi) * gain

    for idx, kind in enumerate(kinds):

        @pl.when(j == idx)
        def _(idx=idx, kind=kind):
            acc = jnp.dot(h_scr[...], w_ref[idx], preferred_element_type=F32)
            for i in range(k):
                rows = slice(i * tl, (i + 1) * tl)
                if kind == "plain":
                    out_ref[0, i] = acc[rows, :].astype(BF16)
                    continue
                for c in range(N_TILES):
                    cols = slice(c * LANES, (c + 1) * LANES)
                    z = head_norm(acc[rows, cols], gain_ref[idx, :, cols])
                    if rotary:
                        tab = slice(i * LANES, (i + 1) * LANES)
                        z = (z * c_ref[0, :, tab]
                             + pltpu.roll(z, ROT_DIM // 2, 1) * s1_ref[0, :, tab]
                             + pltpu.roll(z, LANES - ROT_DIM // 2, 1) * s2_ref[0, :, tab])
                    out_ref[0, i, :, cols] = z.astype(BF16)


def _inproj(x, norm_g, w_sec, gains, kinds, *, r, tabs=None, w_f=None):
    B, S, _ = x.shape
    L = S // r
    k = min(r, ROW_TILE // BLOCK)
    tl = ROW_TILE // k
    nsec = len(kinds)
    rotary = tabs is not None
    with_f = w_f is not None
    xv = x.reshape(B, L, r * D_MODEL)

    row_map = lambda b, lb, kb, j: (b, lb, kb)
    const2 = lambda b, lb, kb, j: (0, 0)
    const3 = lambda b, lb, kb, j: (0, 0, 0)
    in_specs = [
        pl.BlockSpec((1, tl, k * D_MODEL), row_map),
        pl.BlockSpec((1, D_MODEL), const2),
        pl.BlockSpec((nsec, D_MODEL, INNER), const3),
        pl.BlockSpec((nsec, 1, INNER), const3),
    ]
    args = [xv, norm_g.reshape(1, D_MODEL), w_sec, gains]
    if rotary:
        for t in tabs:
            in_specs.append(pl.BlockSpec((1, tl, k * LANES), row_map))
            args.append(t.reshape(B, L, r * LANES))
    if with_f:
        in_specs.append(pl.BlockSpec((D_MODEL, LANES), const2))
        args.append(w_f)
    out_shape = [jax.ShapeDtypeStruct((B, r, L, nsec * INNER), BF16)]
    out_specs = [pl.BlockSpec((1, k, tl, INNER), lambda b, lb, kb, j: (b, kb, lb, j))]
    if with_f:
        out_shape.append(jax.ShapeDtypeStruct((B, LANES, S), F32))
        out_specs.append(pl.BlockSpec((1, LANES, tl), lambda b, lb, kb, j: (b, 0, lb)))
    res = pl.pallas_call(
        functools.partial(_inproj_kernel, k=k, tl=tl, kinds=kinds, rotary=rotary, with_f=with_f),
        grid=(B, L // tl, r // k, nsec),
        in_specs=in_specs,
        out_specs=out_specs,
        out_shape=out_shape,
        scratch_shapes=[pltpu.VMEM((k * tl, D_MODEL), BF16)],
        compiler_params=pltpu.CompilerParams(
            dimension_semantics=("arbitrary",) * 4, vmem_limit_bytes=VMEM_LIMIT),
        name=f"inproj_r{r}",
    )(*args)
    return res


def _fox_c_kernel(ft_ref, bf_ref, c_ref, *, blk):
    x = ft_ref[0] + bf_ref[...]
    ls = jnp.minimum(x, 0.0) - jnp.log1p(jnp.exp(-jnp.abs(x)))
    S = ls.shape[-1]
    tri = (lax.broadcasted_iota(jnp.int32, (blk, blk), 0)
           <= lax.broadcasted_iota(jnp.int32, (blk, blk), 1)).astype(BF16)
    carry = jnp.zeros((N_HEADS, 1), F32)
    for n in range(S // blk):
        t = ls[:, n * blk:(n + 1) * blk]
        h1 = t.astype(BF16)
        r1 = t - h1.astype(F32)
        h2 = r1.astype(BF16)
        h3 = (r1 - h2.astype(F32)).astype(BF16)
        cs = (jnp.dot(h1, tri, preferred_element_type=F32)
              + jnp.dot(h2, tri, preferred_element_type=F32)
              + jnp.dot(h3, tri, preferred_element_type=F32)) + carry
        c_ref[0, :, n * blk:(n + 1) * blk] = cs
        carry = cs[:, blk - 1:blk]


def _fox_c(ft, b_f):
    B, _, S = ft.shape
    return pl.pallas_call(
        functools.partial(_fox_c_kernel, blk=FOX_TK),
        grid=(B,),
        in_specs=[pl.BlockSpec((1, N_HEADS, S), lambda b: (b, 0, 0)),
                  pl.BlockSpec((N_HEADS, 1), lambda b: (0, 0))],
        out_specs=pl.BlockSpec((1, N_HEADS, S), lambda b: (b, 0, 0)),
        out_shape=jax.ShapeDtypeStruct((B, N_HEADS, S), F32),
        compiler_params=pltpu.CompilerParams(dimension_semantics=("arbitrary",)),
        name="fox_c",
    )(ft, b_f.reshape(N_HEADS, 1))


def _silu(g):
    return g * jax.nn.sigmoid(g)


def _fox_attn_kernel(q_ref, k_ref, v_ref, g_ref, c_ref, o_ref, m_scr, l_scr, acc_scr, *, tq, tk):
    hp = pl.program_id(1)
    qi = pl.program_id(2)
    nk = k_ref.shape[1] // tk
    qf = q_ref[0].astype(F32)
    low = _lane_is_low(qf.shape)
    causal = (lax.broadcasted_iota(jnp.int32, (tq, tk), 1)
              <= lax.broadcasted_iota(jnp.int32, (tq, tk), 0))
    outs = []
    for hh in range(HEADS_PER_TILE):
        mine = low if hh == 0 else jnp.logical_not(low)
        qm = jnp.where(mine, qf, 0.0).astype(BF16)
        crow = (hp * HEADS_PER_TILE + hh) * nk
        c0 = c_ref[0, crow + qi][:, 0:1]
        m_scr[...] = jnp.full(m_scr.shape, NEG, F32)
        l_scr[...] = jnp.zeros(l_scr.shape, F32)
        acc_scr[...] = jnp.zeros(acc_scr.shape, F32)

        def tile(j, masked, qm=qm, crow=crow, c0=c0):
            ks = pl.multiple_of(j * tk, tk)
            kt = k_ref[0, pl.ds(ks, tk), :]
            vt = v_ref[0, pl.ds(ks, tk), :]
            s = lax.dot_general(qm, kt, (((1,), (1,)), ((), ())), preferred_element_type=F32)
            s = s + (c0 - c_ref[0, crow + j])
            if masked:
                s = jnp.where(causal, s, NEG)
            m_prev = m_scr[...]
            m_new = jnp.maximum(m_prev, jnp.max(s, axis=-1, keepdims=True))
            alpha = jnp.exp(m_prev - m_new)
            p = jnp.exp(s - m_new)
            l_scr[...] = alpha * l_scr[...] + jnp.sum(p, axis=-1, keepdims=True)
            acc_scr[...] = alpha * acc_scr[...] + jnp.dot(
                p.astype(BF16), vt, preferred_element_type=F32)
            m_scr[...] = m_new

        def body(j, carry):
            tile(j, False)
            return carry

        lax.fori_loop(0, qi, body, 0)
        tile(qi, True)
        outs.append(acc_scr[...] / l_scr[...])
    o = jnp.where(low, outs[0], outs[1])
    o_ref[0] = (o * _silu(g_ref[0].astype(F32))).astype(BF16)


def _fox_attn(qkvg, c):
    B, S, _ = qkvg.shape
    tq, tk = FOX_TQ, FOX_TK
    c2 = c.reshape(B, N_HEADS * (S // tk), 1, tk)
    return pl.pallas_call(
        functools.partial(_fox_attn_kernel, tq=tq, tk=tk),
        grid=(B, N_TILES, S // tq),
        in_specs=[
            pl.BlockSpec((1, tq, LANES), lambda b, hp, qi: (b, qi, hp)),
            pl.BlockSpec((1, S, LANES), lambda b, hp, qi: (b, 0, N_TILES + hp)),
            pl.BlockSpec((1, S, LANES), lambda b, hp, qi: (b, 0, 2 * N_TILES + hp)),
            pl.BlockSpec((1, tq, LANES), lambda b, hp, qi: (b, qi, 3 * N_TILES + hp)),
            pl.BlockSpec((1, N_HEADS * (S // tk), 1, tk), lambda b, hp, qi: (b, 0, 0, 0)),
        ],
        out_specs=pl.BlockSpec((1, tq, LANES), lambda b, hp, qi: (b, qi, hp)),
        out_shape=jax.ShapeDtypeStruct((B, S, INNER), BF16),
        scratch_shapes=[pltpu.VMEM((tq, 1), F32), pltpu.VMEM((tq, 1), F32),
                        pltpu.VMEM((tq, LANES), F32)],
        compiler_params=pltpu.CompilerParams(
            dimension_semantics=("arbitrary",) * 3, vmem_limit_bytes=VMEM_LIMIT),
        name="fox_attn",
    )(qkvg, qkvg, qkvg, qkvg, c2)


def _dsw_attn_kernel(*refs):
    (q0, k0, v0, gate_ref, q1, k1, v1, q2, k2, v2, o_ref, og_scr, lse_scr) = refs
    groups = ((q0, k0, v0), (q1, k1, v1), (q2, k2, v2))
    S = o_ref.shape[1]
    low = _lane_is_low((BLOCK, LANES))
    row = lax.broadcasted_iota(jnp.int32, (BLOCK, 2 * BLOCK), 0)
    col = lax.broadcasted_iota(jnp.int32, (BLOCK, 2 * BLOCK), 1)
    band = (col >= row) & (col <= row + BLOCK)
    band0 = band & (col >= BLOCK)

    for g, (_, r) in enumerate(DSWA_GROUPS):
        q_ref, k_ref, v_ref = groups[g]
        nb = S // r // BLOCK
        for z in range(r):
            for n in range(nb):
                cur = slice(n * BLOCK, (n + 1) * BLOCK)
                prev = slice(max(n - 1, 0) * BLOCK, (max(n - 1, 0) + 1) * BLOCK)
                qf = q_ref[0, z, cur, :].astype(F32)
                kb = jnp.concatenate([k_ref[0, z, prev, :], k_ref[0, z, cur, :]], axis=0)
                vb = jnp.concatenate([v_ref[0, z, prev, :], v_ref[0, z, cur, :]], axis=0)
                valid = band if n > 0 else band0
                o_h, lse_h = [], []
                for hh in range(HEADS_PER_TILE):
                    mine = low if hh == 0 else jnp.logical_not(low)
                    qm = jnp.where(mine, qf, 0.0).astype(BF16)
                    s = lax.dot_general(qm, kb, (((1,), (1,)), ((), ())),
                                        preferred_element_type=F32)
                    s = jnp.where(valid, s, NEG)
                    m = jnp.max(s, axis=-1, keepdims=True)
                    p = jnp.exp(s - m)
                    l = jnp.sum(p, axis=-1, keepdims=True)
                    o_h.append(jnp.dot(p.astype(BF16), vb, preferred_element_type=F32) / l)
                    lse_h.append(m + jnp.log(l))
                dst = pl.ds(z + r * n * BLOCK, BLOCK, stride=r) if r > 1 else cur
                og_scr[g, dst, :] = jnp.where(low, o_h[0], o_h[1])
                lse_scr[g, dst, :] = jnp.where(low, lse_h[0], lse_h[1])

    chunk = 2 * BLOCK

    def combine(i, carry):
        rows = pl.ds(pl.multiple_of(i * chunk, chunk), chunk)
        l0, l1, l2 = lse_scr[0, rows, :], lse_scr[1, rows, :], lse_scr[2, rows, :]
        mx = jnp.maximum(jnp.maximum(l0, l1), l2)
        e0, e1, e2 = jnp.exp(l0 - mx), jnp.exp(l1 - mx), jnp.exp(l2 - mx)
        o = (e0 * og_scr[0, rows, :] + e1 * og_scr[1, rows, :] + e2 * og_scr[2, rows, :]) / (
            e0 + e1 + e2)
        o_ref[0, rows, :] = (o * _silu(gate_ref[0, 0, rows, :].astype(F32))).astype(BF16)
        return carry

    lax.fori_loop(0, S // chunk, combine, 0)


def _dsw_attn(p0, p1, p2):
    B, _, S, _ = p0.shape
    in_specs, args = [], []
    for g, (p, (_, r)) in enumerate(zip((p0, p1, p2), DSWA_GROUPS)):
        for sec in range(3):
            in_specs.append(pl.BlockSpec(
                (1, r, S // r, LANES), lambda b, hp, sec=sec: (b, 0, 0, sec * N_TILES + hp)))
            args.append(p)
        if g == 0:
            in_specs.append(pl.BlockSpec(
                (1, 1, S, LANES), lambda b, hp: (b, 0, 0, 3 * N_TILES + hp)))
            args.append(p)
    return pl.pallas_call(
        _dsw_attn_kernel,
        grid=(B, N_TILES),
        in_specs=in_specs,
        out_specs=pl.BlockSpec((1, S, LANES), lambda b, hp: (b, 0, hp)),
        out_shape=jax.ShapeDtypeStruct((B, S, INNER), BF16),
        scratch_shapes=[pltpu.VMEM((3, S, LANES), F32), pltpu.VMEM((3, S, LANES), F32)],
        compiler_params=pltpu.CompilerParams(
            dimension_semantics=("arbitrary",) * 2, vmem_limit_bytes=VMEM_LIMIT),
        name="dsw_attn",
    )(*args)


def _outproj_kernel(a_ref, w_ref, x_ref, o_ref):
    o_ref[...] = x_ref[...] + jnp.dot(a_ref[...], w_ref[...], preferred_element_type=F32)


def _outproj(a, w, x):
    M = a.shape[0]
    return pl.pallas_call(
        _outproj_kernel,
        grid=(M // ROW_TILE,),
        in_specs=[pl.BlockSpec((ROW_TILE, INNER), lambda i: (i, 0)),
                  pl.BlockSpec((INNER, D_MODEL), lambda i: (0, 0)),
                  pl.BlockSpec((ROW_TILE, D_MODEL), lambda i: (i, 0))],
        out_specs=pl.BlockSpec((ROW_TILE, D_MODEL), lambda i: (i, 0)),
        out_shape=jax.ShapeDtypeStruct((M, D_MODEL), F32),
        compiler_params=pltpu.CompilerParams(
            dimension_semantics=("arbitrary",), vmem_limit_bytes=VMEM_LIMIT),
        name="outproj",
    )(a, w, x)


def _sections(w_in, idxs):
    n = w_in.shape[1] // INNER
    w = w_in[:, :n * INNER].reshape(D_MODEL, n, INNER)
    return jnp.stack([w[:, i, :] for i in idxs], axis=0).astype(BF16)


def _gain_rows(q_g, k_g, n_plain):
    rows = [jnp.tile(q_g * SCALE, N_HEADS), jnp.tile(k_g, N_HEADS)]
    rows += [jnp.ones((INNER,), F32)] * n_plain
    return jnp.stack(rows, axis=0).reshape(len(rows), 1, INNER).astype(F32)


def _rotary_tables(positions):
    half = ROT_DIM // 2
    inv_freq = ROPE_THETA ** (-jnp.arange(0, ROT_DIM, 2, dtype=F32) / ROT_DIM)
    ang = positions.astype(F32)[..., None] * inv_freq
    cos, sin = jnp.cos(ang), jnp.sin(ang)
    B, S = positions.shape
    one = jnp.ones((B, S, HEAD_DIM - ROT_DIM), F32)
    zero8 = jnp.zeros((B, S, half), F32)
    zero = jnp.zeros((B, S, HEAD_DIM - ROT_DIM), F32)
    c = jnp.concatenate([cos, cos, one], axis=-1)
    s1 = jnp.concatenate([zero8, sin, zero], axis=-1)
    s2 = jnp.concatenate([-sin, zero8, zero], axis=-1)
    return tuple(jnp.tile(t, (1, 1, HEADS_PER_TILE)) for t in (c, s1, s2))


def kernel(x, positions, norm_g, fox_w_in, fox_b_f, fox_q_norm, fox_k_norm, fox_w_out,
           dsw_w_in, dsw_q_norm, dsw_k_norm, dsw_w_out):
    B, S, _ = x.shape
    depth = norm_g.shape[0]
    tabs = _rotary_tables(positions)
    for i in range(depth):
        j = i // 2
        if i % 2 == 0:
            w_sec = _sections(fox_w_in[j], (0, 1, 2, 3))
            w_f = jnp.pad(fox_w_in[j][:, 4 * INNER:], ((0, 0), (0, LANES - N_HEADS))).astype(BF16)
            gains = _gain_rows(fox_q_norm[j], fox_k_norm[j], 2)
            qkvg, ft = _inproj(x, norm_g[i], w_sec, gains,
                               ("norm", "norm", "plain", "plain"), r=1, w_f=w_f)
            c = _fox_c(ft, fox_b_f[j])
            a = _fox_attn(qkvg.reshape(B, S, 4 * INNER), c)
            w_out = fox_w_out[j]
        else:
            projs = []
            for g, (_, r) in enumerate(DSWA_GROUPS):
                idxs = (g, 3 + g, 6 + g) + ((9,) if g == 0 else ())
                kinds = ("norm", "norm", "plain") + (("plain",) if g == 0 else ())
                gains = _gain_rows(dsw_q_norm[j, g], dsw_k_norm[j, g], len(kinds) - 2)
                projs.append(_inproj(x, norm_g[i], _sections(dsw_w_in[j], idxs), gains, kinds,
                                     r=r, tabs=tabs)[0])
            a = _dsw_attn(*projs)
            w_out = dsw_w_out[j]
        x = _outproj(a.reshape(B * S, INNER), w_out.astype(BF16),
                     x.reshape(B * S, D_MODEL)).reshape(B, S, D_MODEL)
    return x
```

```python
import functools

import jax
import jax.numpy as jnp
import numpy as np
from jax import lax
from jax.experimental import pallas as pl
from jax.experimental.pallas import tpu as pltpu

D_MODEL = 1024
HEAD_DIM = 64
N_HEADS = 16
INNER = N_HEADS * HEAD_DIM
ROT_DIM = HEAD_DIM // 4
ROPE_THETA = 500000.0
DSWA_GROUPS = ((128, 1), (512, 4), (2048, 16))
BLOCK = 128
EPS = 1e-6
SCALE = HEAD_DIM ** -0.5

LANES = 128
MXU_DIM = 256
HEADS_PER_TILE = LANES // HEAD_DIM
N_TILES = INNER // LANES
NEG = -1e30
VMEM_LIMIT = 48 * 1024 * 1024

ROW_TILE = 512
FOX_TQ = 512
FOX_SUB = 256

F32 = jnp.float32
BF16 = jnp.bfloat16


def _lane_is_low(shape):
    return lax.broadcasted_iota(jnp.int32, shape, len(shape) - 1) < HEAD_DIM


def _strided_rows(ref, i, n, stride):
    if stride == 1:
        return ref[0, i * n:(i + 1) * n, :]
    return ref[0, pl.ds(i, n, stride=stride), :]


def _inproj_kernel(*refs, r, kinds, rotary, with_f):
    it = iter(refs)
    x_ref, ng_ref, w_ref, gain_ref, bd_ref = (next(it) for _ in range(5))
    if rotary:
        c_ref, s_ref, p_ref = next(it), next(it), next(it)
    if with_f:
        wf_ref = next(it)
    out_ref = next(it)
    if with_f:
        ft_ref = next(it)
    h_scr = next(it)
    if rotary:
        tab_scr = next(it)
    if r > 1:
        hs_scr = next(it)

    tl = ROW_TILE // r
    j = pl.program_id(2)

    @pl.when(j == 0)
    def _():
        x = x_ref[0]
        ms = jnp.mean(x * x, axis=-1, keepdims=True)
        h = x * lax.rsqrt(ms + EPS) * ng_ref[...]
        if r == 1:
            h_scr[...] = h.astype(BF16)
        else:
            for c in range(D_MODEL // LANES):
                hs_scr[c] = h[:, c * LANES:(c + 1) * LANES]
        for i in range(r):
            rows = slice(i * tl, (i + 1) * tl)
            if r > 1:
                for c in range(D_MODEL // LANES):
                    h_scr[rows, c * LANES:(c + 1) * LANES] = hs_scr[
                        c, pl.ds(i, tl, stride=r), :].astype(BF16)
            if rotary:
                tab_scr[0, rows, :] = _strided_rows(c_ref, i, tl, r)
                tab_scr[1, rows, :] = _strided_rows(s_ref, i, tl, r)
        if with_f:
            f = jnp.dot(h_scr[...], wf_ref[...], preferred_element_type=F32)
            ft_ref[0] = f.T

    reps = MXU_DIM // LANES

    for idx, kind in enumerate(kinds):

        @pl.when(j == idx)
        def _(idx=idx, kind=kind):
            for cc in range(INNER // MXU_DIM):
                cols = slice(cc * MXU_DIM, (cc + 1) * MXU_DIM)
                a = jnp.dot(h_scr[...], w_ref[idx, :, cols], preferred_element_type=F32)
                if kind == "norm":
                    ms = jnp.dot((a * a).astype(BF16), bd_ref[...], preferred_element_type=F32)
                    rs = lax.rsqrt(ms + EPS)
                    if rotary:
                        u = a * gain_ref[idx, :, cols]
                        up = jnp.dot(u.astype(BF16), p_ref[...], preferred_element_type=F32)
                        cosv = jnp.concatenate([tab_scr[0]] * reps, axis=1)
                        sinv = jnp.concatenate([tab_scr[1]] * reps, axis=1)
                        a = (u * cosv + up * sinv) * rs
                    else:
                        a = a * rs * gain_ref[idx, :, cols]
                ab = a.astype(BF16)
                for i in range(r):
                    out_ref[0, i, :, cols] = ab[i * tl:(i + 1) * tl, :]


def _head_mean_matrix():
    i = np.arange(MXU_DIM)
    return jnp.asarray((i[:, None] // HEAD_DIM == i[None, :] // HEAD_DIM) / HEAD_DIM, BF16)


def _rotary_partner_matrix():
    half = ROT_DIM // 2
    i = np.arange(MXU_DIM)
    d = i % HEAD_DIM
    partner = np.where(d < half, i + half, np.where(d < ROT_DIM, i - half, -1))
    return jnp.asarray(i[:, None] == partner[None, :], BF16)


def _inproj(x, norm_g, w_sec, gains, kinds, *, r, tabs=None, w_f=None):
    B, S, _ = x.shape
    L = S // r
    tl = ROW_TILE // r
    nsec = len(kinds)
    rotary = tabs is not None
    with_f = w_f is not None

    row_map = lambda b, t, j: (b, t, 0)
    const2 = lambda b, t, j: (0, 0)
    const3 = lambda b, t, j: (0, 0, 0)
    in_specs = [
        pl.BlockSpec((1, ROW_TILE, D_MODEL), row_map),
        pl.BlockSpec((1, D_MODEL), const2),
        pl.BlockSpec((nsec, D_MODEL, INNER), const3),
        pl.BlockSpec((nsec, 1, INNER), const3),
        pl.BlockSpec((MXU_DIM, MXU_DIM), const2),
    ]
    args = [x, norm_g.reshape(1, D_MODEL), w_sec, gains, _head_mean_matrix()]
    scratch = [pltpu.VMEM((ROW_TILE, D_MODEL), BF16)]
    if rotary:
        for t in tabs:
            in_specs.append(pl.BlockSpec((1, ROW_TILE, LANES), row_map))
            args.append(t)
        in_specs.append(pl.BlockSpec((MXU_DIM, MXU_DIM), const2))
        args.append(_rotary_partner_matrix())
        scratch.append(pltpu.VMEM((2, ROW_TILE, LANES), F32))
    if r > 1:
        scratch.append(pltpu.VMEM((D_MODEL // LANES, ROW_TILE, LANES), F32))
    if with_f:
        in_specs.append(pl.BlockSpec((D_MODEL, LANES), const2))
        args.append(w_f)
    out_shape = [jax.ShapeDtypeStruct((B, r, L, nsec * INNER), BF16)]
    out_specs = [pl.BlockSpec((1, r, tl, INNER), lambda b, t, j: (b, 0, t, j))]
    if with_f:
        out_shape.append(jax.ShapeDtypeStruct((B, LANES, S), F32))
        out_specs.append(pl.BlockSpec((1, LANES, ROW_TILE), lambda b, t, j: (b, 0, t)))
    return pl.pallas_call(
        functools.partial(_inproj_kernel, r=r, kinds=kinds, rotary=rotary, with_f=with_f),
        grid=(B, S // ROW_TILE, nsec),
        in_specs=in_specs,
        out_specs=out_specs,
        out_shape=out_shape,
        scratch_shapes=scratch,
        compiler_params=pltpu.CompilerParams(
            dimension_semantics=("arbitrary",) * 3, vmem_limit_bytes=VMEM_LIMIT),
        name=f"inproj_r{r}",
    )(*args)


def _fox_c_kernel(ft_ref, bf_ref, c_ref, *, blk):
    x = ft_ref[0] + bf_ref[...]
    ls = jnp.minimum(x, 0.0) - jnp.log1p(jnp.exp(-jnp.abs(x)))
    S = ls.shape[-1]
    tri = (lax.broadcasted_iota(jnp.int32, (blk, blk), 0)
           <= lax.broadcasted_iota(jnp.int32, (blk, blk), 1)).astype(BF16)
    carry = jnp.zeros((N_HEADS, 1), F32)
    for n in range(S // blk):
        t = ls[:, n * blk:(n + 1) * blk]
        h1 = t.astype(BF16)
        r1 = t - h1.astype(F32)
        h2 = r1.astype(BF16)
        h3 = (r1 - h2.astype(F32)).astype(BF16)
        cs = (jnp.dot(h1, tri, preferred_element_type=F32)
              + jnp.dot(h2, tri, preferred_element_type=F32)
              + jnp.dot(h3, tri, preferred_element_type=F32)) + carry
        c_ref[0, :, n * blk:(n + 1) * blk] = cs
        carry = cs[:, blk - 1:blk]


def _fox_c(ft, b_f):
    B, _, S = ft.shape
    return pl.pallas_call(
        functools.partial(_fox_c_kernel, blk=FOX_SUB),
        grid=(B,),
        in_specs=[pl.BlockSpec((1, N_HEADS, S), lambda b: (b, 0, 0)),
                  pl.BlockSpec((N_HEADS, 1), lambda b: (0, 0))],
        out_specs=pl.BlockSpec((1, N_HEADS, S), lambda b: (b, 0, 0)),
        out_shape=jax.ShapeDtypeStruct((B, N_HEADS, S), F32),
        compiler_params=pltpu.CompilerParams(dimension_semantics=("arbitrary",)),
        name="fox_c",
    )(ft, b_f.reshape(N_HEADS, 1))


def _silu(g):
    return g * jax.nn.sigmoid(g)


def _fox_attn_kernel(q_ref, k_ref, v_ref, g_ref, c_ref, o_ref, m_scr, acc_scr, va_scr,
                     *, tq, sub):
    hp = pl.program_id(1)
    qi = pl.program_id(2)
    S = k_ref.shape[1]
    nsub = S // sub
    per_tile = tq // sub
    low = _lane_is_low((tq, LANES))

    @pl.when(qi == 0)
    def _():
        for n in range(S // tq):
            rows = slice(n * tq, (n + 1) * tq)
            vf = v_ref[0, rows, :].astype(F32)
            va_scr[0, rows, :] = jnp.where(low, vf, 1.0).astype(BF16)
            va_scr[1, rows, :] = jnp.where(low, 1.0, vf).astype(BF16)

    qf = q_ref[0].astype(F32)
    qm = (jnp.where(low, qf, 0.0).astype(BF16), jnp.where(low, 0.0, qf).astype(BF16))
    m_scr[...] = jnp.full(m_scr.shape, NEG, F32)
    acc_scr[...] = jnp.zeros(acc_scr.shape, F32)

    def bias_row(hh, blk0, nblk):
        crow = (hp * HEADS_PER_TILE + hh) * nsub
        c0 = c_ref[0, crow + qi * per_tile][:, 0:1]
        cs = jnp.concatenate([c_ref[0, crow + blk0 + n] for n in range(nblk)], axis=1)
        return c0 - cs

    def update(hh, r0, nr, k0, blk0, nblk, mask):
        rows = slice(r0, r0 + nr)
        w = nblk * sub
        kt = k_ref[0, pl.ds(k0, w), :]
        s = lax.dot_general(qm[hh][rows], kt, (((1,), (1,)), ((), ())),
                            preferred_element_type=F32)
        s = s + bias_row(hh, blk0, nblk)
        if mask is not None:
            s = jnp.where(mask, s, NEG)
        m_prev = m_scr[hh, rows, :]
        m_new = jnp.maximum(m_prev, jnp.max(s, axis=-1, keepdims=True))
        alpha = jnp.exp(m_prev - m_new)
        p = jnp.exp(s - jnp.concatenate([m_new] * (w // LANES), axis=1))
        acc_scr[hh, rows, :] = alpha * acc_scr[hh, rows, :] + jnp.dot(
            p.astype(BF16), va_scr[hh, pl.ds(k0, w), :], preferred_element_type=F32)
        m_scr[hh, rows, :] = m_new

    def body(j, carry):
        k0 = pl.multiple_of(j * tq, tq)
        for hh in range(HEADS_PER_TILE):
            update(hh, 0, tq, k0, j * per_tile, per_tile, None)
        return carry

    lax.fori_loop(0, qi, body, 0)

    k0 = pl.multiple_of(qi * tq, tq)
    mask0 = (lax.broadcasted_iota(jnp.int32, (tq, sub), 1)
             <= lax.broadcasted_iota(jnp.int32, (tq, sub), 0))
    mask1 = mask0[:sub, :]
    for hh in range(HEADS_PER_TILE):
        update(hh, 0, tq, k0, qi * per_tile, 1, mask0)
        update(hh, sub, tq - sub, pl.multiple_of(k0 + sub, sub), qi * per_tile + 1, 1, mask1)

    outs = []
    for hh in range(HEADS_PER_TILE):
        acc = acc_scr[hh]
        outs.append(acc / pltpu.roll(acc, HEAD_DIM, 1))
    o = jnp.where(low, outs[0], outs[1])
    o_ref[0] = (o * _silu(g_ref[0].astype(F32))).astype(BF16)


def _fox_attn(qkvg, c):
    B, S, _ = qkvg.shape
    tq, sub = FOX_TQ, FOX_SUB
    assert tq == 2 * sub
    nrow = N_HEADS * (S // sub)
    c2 = c.reshape(B, nrow, 1, sub)
    return pl.pallas_call(
        functools.partial(_fox_attn_kernel, tq=tq, sub=sub),
        grid=(B, N_TILES, S // tq),
        in_specs=[
            pl.BlockSpec((1, tq, LANES), lambda b, hp, qi: (b, qi, hp)),
            pl.BlockSpec((1, S, LANES), lambda b, hp, qi: (b, 0, N_TILES + hp)),
            pl.BlockSpec((1, S, LANES), lambda b, hp, qi: (b, 0, 2 * N_TILES + hp)),
            pl.BlockSpec((1, tq, LANES), lambda b, hp, qi: (b, qi, 3 * N_TILES + hp)),
            pl.BlockSpec((1, nrow, 1, sub), lambda b, hp, qi: (b, 0, 0, 0)),
        ],
        out_specs=pl.BlockSpec((1, tq, LANES), lambda b, hp, qi: (b, qi, hp)),
        out_shape=jax.ShapeDtypeStruct((B, S, INNER), BF16),
        scratch_shapes=[pltpu.VMEM((HEADS_PER_TILE, tq, LANES), F32),
                        pltpu.VMEM((HEADS_PER_TILE, tq, LANES), F32),
                        pltpu.VMEM((HEADS_PER_TILE, S, LANES), BF16)],
        compiler_params=pltpu.CompilerParams(
            dimension_semantics=("arbitrary",) * 3, vmem_limit_bytes=VMEM_LIMIT),
        name="fox_attn",
    )(qkvg, qkvg, qkvg, qkvg, c2)


def _dsw_attn_kernel(*refs):
    (q0, k0, v0, gate_ref, q1, k1, v1, q2, k2, v2, o_ref, og_scr, lse_scr) = refs
    groups = ((q0, k0, v0), (q1, k1, v1), (q2, k2, v2))
    S = o_ref.shape[1]
    low = _lane_is_low((BLOCK, LANES))
    row = lax.broadcasted_iota(jnp.int32, (BLOCK, 2 * BLOCK), 0)
    col = lax.broadcasted_iota(jnp.int32, (BLOCK, 2 * BLOCK), 1)
    band = (col >= row) & (col <= row + BLOCK)
    band0 = band & (col >= BLOCK)

    for g, (_, r) in enumerate(DSWA_GROUPS):
        q_ref, k_ref, v_ref = groups[g]
        nb = S // r // BLOCK
        for z in range(r):
            for n in range(nb):
                cur = slice(n * BLOCK, (n + 1) * BLOCK)
                prev = slice(max(n - 1, 0) * BLOCK, (max(n - 1, 0) + 1) * BLOCK)
                qf = q_ref[0, z, cur, :].astype(F32)
                kb = jnp.concatenate([k_ref[0, z, prev, :], k_ref[0, z, cur, :]], axis=0)
                vb = jnp.concatenate([v_ref[0, z, prev, :], v_ref[0, z, cur, :]], axis=0)
                valid = band if n > 0 else band0
                o_h, lse_h = [], []
                for hh in range(HEADS_PER_TILE):
                    mine = low if hh == 0 else jnp.logical_not(low)
                    qm = jnp.where(mine, qf, 0.0).astype(BF16)
                    s = lax.dot_general(qm, kb, (((1,), (1,)), ((), ())),
                                        preferred_element_type=F32)
                    s = jnp.where(valid, s, NEG)
                    m = jnp.max(s, axis=-1, keepdims=True)
                    p = jnp.exp(s - m)
                    l = jnp.sum(p, axis=-1, keepdims=True)
                    o_h.append(jnp.dot(p.astype(BF16), vb, preferred_element_type=F32) / l)
                    lse_h.append(m + jnp.log(l))
                dst = pl.ds(z + r * n * BLOCK, BLOCK, stride=r) if r > 1 else cur
                og_scr[g, dst, :] = jnp.where(low, o_h[0], o_h[1])
                lse_scr[g, dst, :] = jnp.where(low, lse_h[0], lse_h[1])

    chunk = 2 * BLOCK

    def combine(i, carry):
        rows = pl.ds(pl.multiple_of(i * chunk, chunk), chunk)
        l0, l1, l2 = lse_scr[0, rows, :], lse_scr[1, rows, :], lse_scr[2, rows, :]
        mx = jnp.maximum(jnp.maximum(l0, l1), l2)
        e0, e1, e2 = jnp.exp(l0 - mx), jnp.exp(l1 - mx), jnp.exp(l2 - mx)
        o = (e0 * og_scr[0, rows, :] + e1 * og_scr[1, rows, :] + e2 * og_scr[2, rows, :]) / (
            e0 + e1 + e2)
        o_ref[0, rows, :] = (o * _silu(gate_ref[0, 0, rows, :].astype(F32))).astype(BF16)
        return carry

    lax.fori_loop(0, S // chunk, combine, 0)


def _dsw_attn(p0, p1, p2):
    B, _, S, _ = p0.shape
    in_specs, args = [], []
    for g, (p, (_, r)) in enumerate(zip((p0, p1, p2), DSWA_GROUPS)):
        for sec in range(3):
            in_specs.append(pl.BlockSpec(
                (1, r, S // r, LANES), lambda b, hp, sec=sec: (b, 0, 0, sec * N_TILES + hp)))
            args.append(p)
        if g == 0:
            in_specs.append(pl.BlockSpec(
                (1, 1, S, LANES), lambda b, hp: (b, 0, 0, 3 * N_TILES + hp)))
            args.append(p)
    return pl.pallas_call(
        _dsw_attn_kernel,
        grid=(B, N_TILES),
        in_specs=in_specs,
        out_specs=pl.BlockSpec((1, S, LANES), lambda b, hp: (b, 0, hp)),
        out_shape=jax.ShapeDtypeStruct((B, S, INNER), BF16),
        scratch_shapes=[pltpu.VMEM((3, S, LANES), F32), pltpu.VMEM((3, S, LANES), F32)],
        compiler_params=pltpu.CompilerParams(
            dimension_semantics=("arbitrary",) * 2, vmem_limit_bytes=VMEM_LIMIT),
        name="dsw_attn",
    )(*args)


def _outproj_kernel(a_ref, w_ref, x_ref, o_ref):
    o_ref[...] = x_ref[...] + jnp.dot(a_ref[...], w_ref[...], preferred_element_type=F32)


def _outproj(a, w, x):
    M = a.shape[0]
    return pl.pallas_call(
        _outproj_kernel,
        grid=(M // ROW_TILE,),
        in_specs=[pl.BlockSpec((ROW_TILE, INNER), lambda i: (i, 0)),
                  pl.BlockSpec((INNER, D_MODEL), lambda i: (0, 0)),
                  pl.BlockSpec((ROW_TILE, D_MODEL), lambda i: (i, 0))],
        out_specs=pl.BlockSpec((ROW_TILE, D_MODEL), lambda i: (i, 0)),
        out_shape=jax.ShapeDtypeStruct((M, D_MODEL), F32),
        compiler_params=pltpu.CompilerParams(
            dimension_semantics=("arbitrary",), vmem_limit_bytes=VMEM_LIMIT),
        name="outproj",
    )(a, w, x)


def _sections(w_in, idxs):
    n = w_in.shape[1] // INNER
    w = w_in[:, :n * INNER].reshape(D_MODEL, n, INNER)
    return jnp.stack([w[:, i, :] for i in idxs], axis=0).astype(BF16)


def _gain_rows(q_g, k_g, n_plain):
    rows = [jnp.tile(q_g * SCALE, N_HEADS), jnp.tile(k_g, N_HEADS)]
    rows += [jnp.ones((INNER,), F32)] * n_plain
    return jnp.stack(rows, axis=0).reshape(len(rows), 1, INNER).astype(F32)


def _rotary_tables(positions):
    half = ROT_DIM // 2
    inv_freq = ROPE_THETA ** (-jnp.arange(0, ROT_DIM, 2, dtype=F32) / ROT_DIM)
    ang = positions.astype(F32)[..., None] * inv_freq
    cos, sin = jnp.cos(ang), jnp.sin(ang)
    B, S = positions.shape
    one = jnp.ones((B, S, HEAD_DIM - ROT_DIM), F32)
    zero = jnp.zeros((B, S, HEAD_DIM - ROT_DIM), F32)
    c = jnp.concatenate([cos, cos, one], axis=-1)
    s = jnp.concatenate([-sin, sin, zero], axis=-1)
    return tuple(jnp.tile(t, (1, 1, HEADS_PER_TILE)) for t in (c, s))


def kernel(x, positions, norm_g, fox_w_in, fox_b_f, fox_q_norm, fox_k_norm, fox_w_out,
           dsw_w_in, dsw_q_norm, dsw_k_norm, dsw_w_out):
    B, S, _ = x.shape
    depth = norm_g.shape[0]
    tabs = _rotary_tables(positions)
    for i in range(depth):
        j = i // 2
        if i % 2 == 0:
            w_sec = _sections(fox_w_in[j], (0, 1, 2, 3))
            w_f = jnp.pad(fox_w_in[j][:, 4 * INNER:], ((0, 0), (0, LANES - N_HEADS))).astype(BF16)
            gains = _gain_rows(fox_q_norm[j], fox_k_norm[j], 2)
            qkvg, ft = _inproj(x, norm_g[i], w_sec, gains,
                               ("norm", "norm", "plain", "plain"), r=1, w_f=w_f)
            c = _fox_c(ft, fox_b_f[j])
            a = _fox_attn(qkvg.reshape(B, S, 4 * INNER), c)
            w_out = fox_w_out[j]
        else:
            projs = []
            for g, (_, r) in enumerate(DSWA_GROUPS):
                idxs = (g, 3 + g, 6 + g) + ((9,) if g == 0 else ())
                kinds = ("norm", "norm", "plain") + (("plain",) if g == 0 else ())
                gains = _gain_rows(dsw_q_norm[j, g], dsw_k_norm[j, g], len(kinds) - 2)
                projs.append(_inproj(x, norm_g[i], _sections(dsw_w_in[j], idxs), gains, kinds,
                                     r=r, tabs=tabs)[0])
            a = _dsw_attn(*projs)
            w_out = dsw_w_out[j]
        x = _outproj(a.reshape(B * S, INNER), w_out.astype(BF16),
                     x.reshape(B * S, D_MODEL)).reshape(B, S, D_MODEL)
    return x
```

```python
import functools

import jax
import jax.numpy as jnp
import numpy as np
from jax import lax
from jax.experimental import pallas as pl
from jax.experimental.pallas import tpu as pltpu

D_MODEL = 1024
HEAD_DIM = 64
N_HEADS = 16
INNER = N_HEADS * HEAD_DIM
ROT_DIM = HEAD_DIM // 4
ROPE_THETA = 500000.0
DSWA_GROUPS = ((128, 1), (512, 4), (2048, 16))
BLOCK = 128
EPS = 1e-6
SCALE = HEAD_DIM ** -0.5

LANES = 128
MXU_DIM = 256
HEADS_PER_TILE = LANES // HEAD_DIM
N_TILES = INNER // LANES
NEG = -1e30
VMEM_LIMIT = 48 * 1024 * 1024

ROW_TILE = 512
FOX_TQ = 512
FOX_SUB = 256

F32 = jnp.float32
BF16 = jnp.bfloat16


def _lane_is_low(shape):
    return lax.broadcasted_iota(jnp.int32, shape, len(shape) - 1) < HEAD_DIM


def _strided_rows(ref, i, n, stride):
    if stride == 1:
        return ref[0, i * n:(i + 1) * n, :]
    return ref[0, pl.ds(i, n, stride=stride), :]


def _inproj_kernel(*refs, r, kinds, rotary, with_f):
    it = iter(refs)
    x_ref, ng_ref, gain_ref, bd_ref = (next(it) for _ in range(4))
    w_refs = [next(it) for _ in kinds]
    if rotary:
        c_ref, s_ref, p_ref = next(it), next(it), next(it)
    if with_f:
        wf_ref = next(it)
    out_ref = next(it)
    if with_f:
        ft_ref = next(it)
    h_scr = next(it)
    if rotary:
        tab_scr = next(it)
    if r > 1:
        hs_scr = next(it)

    tl = ROW_TILE // r
    j = pl.program_id(2)

    @pl.when(j == 0)
    def _():
        x = x_ref[0]
        ms = jnp.mean(x * x, axis=-1, keepdims=True)
        h = x * lax.rsqrt(ms + EPS) * ng_ref[...]
        if r == 1:
            h_scr[...] = h.astype(BF16)
        else:
            for c in range(D_MODEL // LANES):
                hs_scr[c] = h[:, c * LANES:(c + 1) * LANES]
        for i in range(r):
            rows = slice(i * tl, (i + 1) * tl)
            if r > 1:
                for c in range(D_MODEL // LANES):
                    h_scr[rows, c * LANES:(c + 1) * LANES] = hs_scr[
                        c, pl.ds(i, tl, stride=r), :].astype(BF16)
            if rotary:
                tab_scr[0, rows, :] = _strided_rows(c_ref, i, tl, r)
                tab_scr[1, rows, :] = _strided_rows(s_ref, i, tl, r)
        if with_f:
            f = jnp.dot(h_scr[...], wf_ref[...], preferred_element_type=F32)
            ft_ref[0] = f.T

    reps = MXU_DIM // LANES

    for idx, kind in enumerate(kinds):

        @pl.when(j == idx)
        def _(idx=idx, kind=kind):
            nchunk = INNER // MXU_DIM

            def main(cc):
                return jnp.dot(h_scr[...], w_refs[idx][:, cc * MXU_DIM:(cc + 1) * MXU_DIM],
                               preferred_element_type=F32)

            nxt = main(0)
            for cc in range(nchunk):
                cols = slice(cc * MXU_DIM, (cc + 1) * MXU_DIM)
                a = nxt
                if cc + 1 < nchunk:
                    nxt = main(cc + 1)
                if kind == "norm":
                    ms = jnp.dot((a * a).astype(BF16), bd_ref[...], preferred_element_type=F32)
                    rs = lax.rsqrt(ms + EPS)
                    if rotary:
                        u = a * gain_ref[idx, :, cols]
                        up = jnp.dot(u.astype(BF16), p_ref[...], preferred_element_type=F32)
                        cosv = jnp.concatenate([tab_scr[0]] * reps, axis=1)
                        sinv = jnp.concatenate([tab_scr[1]] * reps, axis=1)
                        a = (u * cosv + up * sinv) * rs
                    else:
                        a = a * rs * gain_ref[idx, :, cols]
                ab = a.astype(BF16)
                for i in range(r):
                    out_ref[0, i, :, cols] = ab[i * tl:(i + 1) * tl, :]


def _head_mean_matrix():
    i = np.arange(MXU_DIM)
    return jnp.asarray((i[:, None] // HEAD_DIM == i[None, :] // HEAD_DIM) / HEAD_DIM, BF16)


def _rotary_partner_matrix():
    half = ROT_DIM // 2
    i = np.arange(MXU_DIM)
    d = i % HEAD_DIM
    partner = np.where(d < half, i + half, np.where(d < ROT_DIM, i - half, -1))
    return jnp.asarray(i[:, None] == partner[None, :], BF16)


def _inproj(x, norm_g, w, sec_idx, gains, kinds, *, r, tabs=None, w_f=None):
    B, S, _ = x.shape
    L = S // r
    tl = ROW_TILE // r
    nsec = len(kinds)
    rotary = tabs is not None
    with_f = w_f is not None

    row_map = lambda b, t, j: (b, t, 0)
    const2 = lambda b, t, j: (0, 0)
    const3 = lambda b, t, j: (0, 0, 0)
    in_specs = [
        pl.BlockSpec((1, ROW_TILE, D_MODEL), row_map),
        pl.BlockSpec((1, D_MODEL), const2),
        pl.BlockSpec((nsec, 1, INNER), const3),
        pl.BlockSpec((MXU_DIM, MXU_DIM), const2),
    ]
    args = [x, norm_g.reshape(1, D_MODEL), gains, _head_mean_matrix()]
    for s in sec_idx:
        in_specs.append(pl.BlockSpec((D_MODEL, INNER), lambda b, t, j, s=s: (0, s)))
        args.append(w)
    scratch = [pltpu.VMEM((ROW_TILE, D_MODEL), BF16)]
    if rotary:
        for t in tabs:
            in_specs.append(pl.BlockSpec((1, ROW_TILE, LANES), row_map))
            args.append(t)
        in_specs.append(pl.BlockSpec((MXU_DIM, MXU_DIM), const2))
        args.append(_rotary_partner_matrix())
        scratch.append(pltpu.VMEM((2, ROW_TILE, LANES), F32))
    if r > 1:
        scratch.append(pltpu.VMEM((D_MODEL // LANES, ROW_TILE, LANES), F32))
    if with_f:
        in_specs.append(pl.BlockSpec((D_MODEL, LANES), const2))
        args.append(w_f)
    out_shape = [jax.ShapeDtypeStruct((B, r, L, nsec * INNER), BF16)]
    out_specs = [pl.BlockSpec((1, r, tl, INNER), lambda b, t, j: (b, 0, t, j))]
    if with_f:
        out_shape.append(jax.ShapeDtypeStruct((B, LANES, S), F32))
        out_specs.append(pl.BlockSpec((1, LANES, ROW_TILE), lambda b, t, j: (b, 0, t)))
    return pl.pallas_call(
        functools.partial(_inproj_kernel, r=r, kinds=kinds, rotary=rotary, with_f=with_f),
        grid=(B, S // ROW_TILE, nsec),
        in_specs=in_specs,
        out_specs=out_specs,
        out_shape=out_shape,
        scratch_shapes=scratch,
        compiler_params=pltpu.CompilerParams(
            dimension_semantics=("arbitrary",) * 3, vmem_limit_bytes=VMEM_LIMIT),
        name=f"inproj_r{r}",
    )(*args)


def _fox_c_kernel(ft_ref, bf_ref, c_ref, *, blk):
    x = ft_ref[0] + bf_ref[...]
    ls = jnp.minimum(x, 0.0) - jnp.log1p(jnp.exp(-jnp.abs(x)))
    S = ls.shape[-1]
    tri = (lax.broadcasted_iota(jnp.int32, (blk, blk), 0)
           <= lax.broadcasted_iota(jnp.int32, (blk, blk), 1)).astype(BF16)
    carry = jnp.zeros((N_HEADS, 1), F32)
    for n in range(S // blk):
        t = ls[:, n * blk:(n + 1) * blk]
        h1 = t.astype(BF16)
        r1 = t - h1.astype(F32)
        h2 = r1.astype(BF16)
        h3 = (r1 - h2.astype(F32)).astype(BF16)
        cs = (jnp.dot(h1, tri, preferred_element_type=F32)
              + jnp.dot(h2, tri, preferred_element_type=F32)
              + jnp.dot(h3, tri, preferred_element_type=F32)) + carry
        c_ref[0, :, n * blk:(n + 1) * blk] = cs
        carry = cs[:, blk - 1:blk]


def _fox_c(ft, b_f):
    B, _, S = ft.shape
    return pl.pallas_call(
        functools.partial(_fox_c_kernel, blk=FOX_SUB),
        grid=(B,),
        in_specs=[pl.BlockSpec((1, N_HEADS, S), lambda b: (b, 0, 0)),
                  pl.BlockSpec((N_HEADS, 1), lambda b: (0, 0))],
        out_specs=pl.BlockSpec((1, N_HEADS, S), lambda b: (b, 0, 0)),
        out_shape=jax.ShapeDtypeStruct((B, N_HEADS, S), F32),
        compiler_params=pltpu.CompilerParams(dimension_semantics=("arbitrary",)),
        name="fox_c",
    )(ft, b_f.reshape(N_HEADS, 1))


def _silu(g):
    return g * jax.nn.sigmoid(g)


def _fox_attn_kernel(q_ref, k_ref, v_ref, g_ref, c_ref, o_ref, m_scr, acc_scr, va_scr,
                     *, tq, sub):
    hp = pl.program_id(1)
    qi = pl.program_id(2)
    S = k_ref.shape[1]
    nsub = S // sub
    per_tile = tq // sub
    low = _lane_is_low((tq, LANES))

    @pl.when(qi == 0)
    def _():
        for n in range(S // tq):
            rows = slice(n * tq, (n + 1) * tq)
            vf = v_ref[0, rows, :].astype(F32)
            va_scr[0, rows, :] = jnp.where(low, vf, 1.0).astype(BF16)
            va_scr[1, rows, :] = jnp.where(low, 1.0, vf).astype(BF16)

    qf = q_ref[0].astype(F32)
    qm = (jnp.where(low, qf, 0.0).astype(BF16), jnp.where(low, 0.0, qf).astype(BF16))
    m_scr[...] = jnp.full(m_scr.shape, NEG, F32)
    acc_scr[...] = jnp.zeros(acc_scr.shape, F32)

    def bias_row(hh, blk0, nblk):
        crow = (hp * HEADS_PER_TILE + hh) * nsub
        c0 = c_ref[0, crow + qi * per_tile][:, 0:1]
        cs = jnp.concatenate([c_ref[0, crow + blk0 + n] for n in range(nblk)], axis=1)
        return c0 - cs

    def scores(hh, r0, nr, k0, blk0, nblk, mask):
        kt = k_ref[0, pl.ds(k0, nblk * sub), :]
        s = lax.dot_general(qm[hh][r0:r0 + nr], kt, (((1,), (1,)), ((), ())),
                            preferred_element_type=F32)
        s = s + bias_row(hh, blk0, nblk)
        if mask is not None:
            s = jnp.where(mask, s, NEG)
        return s

    def accumulate(s, hh, r0, nr, k0, blk0, nblk, mask):
        rows = slice(r0, r0 + nr)
        w = nblk * sub
        m_prev = m_scr[hh, rows, :]
        m_new = jnp.maximum(m_prev, jnp.max(s, axis=-1, keepdims=True))
        alpha = jnp.exp(m_prev - m_new)
        p = jnp.exp(s - jnp.concatenate([m_new] * (w // LANES), axis=1))
        acc_scr[hh, rows, :] = alpha * acc_scr[hh, rows, :] + jnp.dot(
            p.astype(BF16), va_scr[hh, pl.ds(k0, w), :], preferred_element_type=F32)
        m_scr[hh, rows, :] = m_new

    def step(tiles):
        ss = [scores(*t) for t in tiles]
        for s, t in zip(ss, tiles):
            accumulate(s, *t)

    def body(j, carry):
        k0 = pl.multiple_of(j * tq, tq)
        step([(hh, 0, tq, k0, j * per_tile, per_tile, None) for hh in range(HEADS_PER_TILE)])
        return carry

    lax.fori_loop(0, qi, body, 0)

    k0 = pl.multiple_of(qi * tq, tq)
    k1 = pl.multiple_of(k0 + sub, sub)
    mask0 = (lax.broadcasted_iota(jnp.int32, (tq, sub), 1)
             <= lax.broadcasted_iota(jnp.int32, (tq, sub), 0))
    mask1 = mask0[:sub, :]
    step([(hh, 0, tq, k0, qi * per_tile, 1, mask0) for hh in range(HEADS_PER_TILE)]
         + [(hh, sub, tq - sub, k1, qi * per_tile + 1, 1, mask1) for hh in range(HEADS_PER_TILE)])

    outs = []
    for hh in range(HEADS_PER_TILE):
        acc = acc_scr[hh]
        outs.append(acc / pltpu.roll(acc, HEAD_DIM, 1))
    o = jnp.where(low, outs[0], outs[1])
    o_ref[0] = (o * _silu(g_ref[0].astype(F32))).astype(BF16)


def _fox_attn(qkvg, c):
    B, S, _ = qkvg.shape
    tq, sub = FOX_TQ, FOX_SUB
    assert tq == 2 * sub
    nrow = N_HEADS * (S // sub)
    c2 = c.reshape(B, nrow, 1, sub)
    return pl.pallas_call(
        functools.partial(_fox_attn_kernel, tq=tq, sub=sub),
        grid=(B, N_TILES, S // tq),
        in_specs=[
            pl.BlockSpec((1, tq, LANES), lambda b, hp, qi: (b, qi, hp)),
            pl.BlockSpec((1, S, LANES), lambda b, hp, qi: (b, 0, N_TILES + hp)),
            pl.BlockSpec((1, S, LANES), lambda b, hp, qi: (b, 0, 2 * N_TILES + hp)),
            pl.BlockSpec((1, tq, LANES), lambda b, hp, qi: (b, qi, 3 * N_TILES + hp)),
            pl.BlockSpec((1, nrow, 1, sub), lambda b, hp, qi: (b, 0, 0, 0)),
        ],
        out_specs=pl.BlockSpec((1, tq, LANES), lambda b, hp, qi: (b, qi, hp)),
        out_shape=jax.ShapeDtypeStruct((B, S, INNER), BF16),
        scratch_shapes=[pltpu.VMEM((HEADS_PER_TILE, tq, LANES), F32),
                        pltpu.VMEM((HEADS_PER_TILE, tq, LANES), F32),
                        pltpu.VMEM((HEADS_PER_TILE, S, LANES), BF16)],
        compiler_params=pltpu.CompilerParams(
            dimension_semantics=("arbitrary",) * 3, vmem_limit_bytes=VMEM_LIMIT),
        name="fox_attn",
    )(qkvg, qkvg, qkvg, qkvg, c2)


def _dsw_attn_kernel(*refs):
    (q0, k0, v0, gate_ref, q1, k1, v1, q2, k2, v2, o_ref, og_scr, lse_scr, *vx_scr) = refs
    groups = ((q0, k0, v0), (q1, k1, v1), (q2, k2, v2))
    S = o_ref.shape[1]
    low = _lane_is_low((BLOCK, LANES))
    row = lax.broadcasted_iota(jnp.int32, (BLOCK, 2 * BLOCK), 0)
    col = lax.broadcasted_iota(jnp.int32, (BLOCK, 2 * BLOCK), 1)
    band = (col >= row) & (col <= row + BLOCK)
    band0 = band & (col >= BLOCK)
    band2 = jnp.concatenate([band] * HEADS_PER_TILE, axis=0)
    band02 = jnp.concatenate([band0] * HEADS_PER_TILE, axis=0)

    def band_slices(n):
        cur = slice(n * BLOCK, (n + 1) * BLOCK)
        prev = slice(max(n - 1, 0) * BLOCK, (max(n - 1, 0) + 1) * BLOCK)
        return prev, cur

    one_lo = jnp.where(low, 1.0, 0.0).astype(BF16)
    one_hi = jnp.where(low, 0.0, 1.0).astype(BF16)
    for g, (_, r) in enumerate(DSWA_GROUPS):
        v_ref = groups[g][2]
        for z in range(r):
            for n in range(S // r // BLOCK):
                cur = slice(n * BLOCK, (n + 1) * BLOCK)
                vf = v_ref[0, z, cur, :].astype(F32)
                vx_scr[g][0, z, cur, :LANES] = jnp.where(low, vf, 0.0).astype(BF16)
                vx_scr[g][0, z, cur, LANES:] = one_lo
                vx_scr[g][1, z, cur, :LANES] = jnp.where(low, 0.0, vf).astype(BF16)
                vx_scr[g][1, z, cur, LANES:] = one_hi

    def block(g, z, n):
        q_ref, k_ref, _ = groups[g]
        r = DSWA_GROUPS[g][1]
        prev, cur = band_slices(n)
        qf = q_ref[0, z, cur, :].astype(F32)
        kb = jnp.concatenate([k_ref[0, z, prev, :], k_ref[0, z, cur, :]], axis=0)
        valid = band2 if n > 0 else band02
        qm = jnp.concatenate([jnp.where(low, qf, 0.0), jnp.where(low, 0.0, qf)],
                             axis=0).astype(BF16)
        s = lax.dot_general(qm, kb, (((1,), (1,)), ((), ())), preferred_element_type=F32)
        s = jnp.where(valid, s, NEG)
        m = jnp.max(s, axis=-1, keepdims=True)
        p = jnp.exp(s - m).astype(BF16)
        vx = jnp.concatenate([vx_scr[g][hh, z, sl, :] for hh in range(HEADS_PER_TILE)
                              for sl in (prev, cur)], axis=0)
        ol = jnp.dot(jnp.concatenate([p[:BLOCK], p[BLOCK:]], axis=1), vx,
                     preferred_element_type=F32)
        l = ol[:, LANES:]
        dst = pl.ds(z + r * n * BLOCK, BLOCK, stride=r) if r > 1 else cur
        og_scr[g, dst, :] = ol[:, :LANES] / l
        lse_scr[g, dst, :] = jnp.where(low, m[:BLOCK], m[BLOCK:]) + jnp.log(l)

    for g, (_, r) in enumerate(DSWA_GROUPS):
        for z in range(r):
            for n in range(S // r // BLOCK):
                block(g, z, n)

    chunk = 2 * BLOCK

    def combine(i, carry):
        rows = pl.ds(pl.multiple_of(i * chunk, chunk), chunk)
        l0, l1, l2 = lse_scr[0, rows, :], lse_scr[1, rows, :], lse_scr[2, rows, :]
        mx = jnp.maximum(jnp.maximum(l0, l1), l2)
        e0, e1, e2 = jnp.exp(l0 - mx), jnp.exp(l1 - mx), jnp.exp(l2 - mx)
        o = (e0 * og_scr[0, rows, :] + e1 * og_scr[1, rows, :] + e2 * og_scr[2, rows, :]) / (
            e0 + e1 + e2)
        o_ref[0, rows, :] = (o * _silu(gate_ref[0, 0, rows, :].astype(F32))).astype(BF16)
        return carry

    lax.fori_loop(0, S // chunk, combine, 0)


def _dsw_attn(p0, p1, p2):
    B, _, S, _ = p0.shape
    in_specs, args = [], []
    for g, (p, (_, r)) in enumerate(zip((p0, p1, p2), DSWA_GROUPS)):
        for sec in range(3):
            in_specs.append(pl.BlockSpec(
                (1, r, S // r, LANES), lambda b, hp, sec=sec: (b, 0, 0, sec * N_TILES + hp)))
            args.append(p)
        if g == 0:
            in_specs.append(pl.BlockSpec(
                (1, 1, S, LANES), lambda b, hp: (b, 0, 0, 3 * N_TILES + hp)))
            args.append(p)
    return pl.pallas_call(
        _dsw_attn_kernel,
        grid=(B, N_TILES),
        in_specs=in_specs,
        out_specs=pl.BlockSpec((1, S, LANES), lambda b, hp: (b, 0, hp)),
        out_shape=jax.ShapeDtypeStruct((B, S, INNER), BF16),
        scratch_shapes=[pltpu.VMEM((3, S, LANES), F32), pltpu.VMEM((3, S, LANES), F32)] + [
            pltpu.VMEM((HEADS_PER_TILE, r, S // r, 2 * LANES), BF16) for _, r in DSWA_GROUPS],
        compiler_params=pltpu.CompilerParams(
            dimension_semantics=("arbitrary",) * 2, vmem_limit_bytes=VMEM_LIMIT),
        name="dsw_attn",
    )(*args)


def _outproj_kernel(a_ref, w_ref, x_ref, o_ref):
    o_ref[...] = x_ref[...] + jnp.dot(a_ref[...], w_ref[...], preferred_element_type=F32)


def _outproj(a, w, x):
    M = a.shape[0]
    return pl.pallas_call(
        _outproj_kernel,
        grid=(M // ROW_TILE,),
        in_specs=[pl.BlockSpec((ROW_TILE, INNER), lambda i: (i, 0)),
                  pl.BlockSpec((INNER, D_MODEL), lambda i: (0, 0)),
                  pl.BlockSpec((ROW_TILE, D_MODEL), lambda i: (i, 0))],
        out_specs=pl.BlockSpec((ROW_TILE, D_MODEL), lambda i: (i, 0)),
        out_shape=jax.ShapeDtypeStruct((M, D_MODEL), F32),
        compiler_params=pltpu.CompilerParams(
            dimension_semantics=("arbitrary",), vmem_limit_bytes=VMEM_LIMIT),
        name="outproj",
    )(a, w, x)


def _gain_rows(q_g, k_g, n_plain):
    rows = [jnp.tile(q_g * SCALE, N_HEADS), jnp.tile(k_g, N_HEADS)]
    rows += [jnp.ones((INNER,), F32)] * n_plain
    return jnp.stack(rows, axis=0).reshape(len(rows), 1, INNER).astype(F32)


def _rotary_tables(positions):
    half = ROT_DIM // 2
    inv_freq = ROPE_THETA ** (-jnp.arange(0, ROT_DIM, 2, dtype=F32) / ROT_DIM)
    ang = positions.astype(F32)[..., None] * inv_freq
    cos, sin = jnp.cos(ang), jnp.sin(ang)
    B, S = positions.shape
    one = jnp.ones((B, S, HEAD_DIM - ROT_DIM), F32)
    zero = jnp.zeros((B, S, HEAD_DIM - ROT_DIM), F32)
    c = jnp.concatenate([cos, cos, one], axis=-1)
    s = jnp.concatenate([-sin, sin, zero], axis=-1)
    return tuple(jnp.tile(t, (1, 1, HEADS_PER_TILE)) for t in (c, s))


def kernel(x, positions, norm_g, fox_w_in, fox_b_f, fox_q_norm, fox_k_norm, fox_w_out,
           dsw_w_in, dsw_q_norm, dsw_k_norm, dsw_w_out):
    B, S, _ = x.shape
    depth = norm_g.shape[0]
    tabs = _rotary_tables(positions)
    for i in range(depth):
        j = i // 2
        if i % 2 == 0:
            w = fox_w_in[j].astype(BF16)
            w_f = jnp.pad(w[:, 4 * INNER:], ((0, 0), (0, LANES - N_HEADS)))
            gains = _gain_rows(fox_q_norm[j], fox_k_norm[j], 2)
            qkvg, ft = _inproj(x, norm_g[i], w, (0, 1, 2, 3), gains,
                               ("norm", "norm", "plain", "plain"), r=1, w_f=w_f)
            c = _fox_c(ft, fox_b_f[j])
            a = _fox_attn(qkvg.reshape(B, S, 4 * INNER), c)
            w_out = fox_w_out[j]
        else:
            projs = []
            w = dsw_w_in[j].astype(BF16)
            n_groups = len(DSWA_GROUPS)
            for g, (_, r) in enumerate(DSWA_GROUPS):
                idxs = (g, n_groups + g, 2 * n_groups + g) + ((3 * n_groups,) if g == 0 else ())
                kinds = ("norm", "norm", "plain") + (("plain",) if g == 0 else ())
                gains = _gain_rows(dsw_q_norm[j, g], dsw_k_norm[j, g], len(kinds) - 2)
                projs.append(_inproj(x, norm_g[i], w, idxs, gains, kinds, r=r, tabs=tabs)[0])
            a = _dsw_attn(*projs)
            w_out = dsw_w_out[j]
        x = _outproj(a.reshape(B * S, INNER), w_out.astype(BF16),
                     x.reshape(B * S, D_MODEL)).reshape(B, S, D_MODEL)
    return x
```

```python
import functools

import jax
import jax.numpy as jnp
import numpy as np
from jax import lax
from jax.experimental import pallas as pl
from jax.experimental.pallas import tpu as pltpu

D_MODEL = 1024
HEAD_DIM = 64
N_HEADS = 16
INNER = N_HEADS * HEAD_DIM
ROT_DIM = HEAD_DIM // 4
ROPE_THETA = 500000.0
DSWA_GROUPS = ((128, 1), (512, 4), (2048, 16))
BLOCK = 128
EPS = 1e-6
SCALE = HEAD_DIM ** -0.5
LOG2E = 1.4426950408889634

LANES = 128
MXU_DIM = 256
HEADS_PER_TILE = LANES // HEAD_DIM
N_TILES = INNER // LANES
NEG = -1e30
VMEM_LIMIT = 48 * 1024 * 1024

ROW_TILE = 512
FOX_TQ = 512
FOX_SUB = 256

F32 = jnp.float32
BF16 = jnp.bfloat16


def _lane_is_low(shape):
    return lax.broadcasted_iota(jnp.int32, shape, len(shape) - 1) < HEAD_DIM


def _strided_rows(ref, i, n, stride):
    if stride == 1:
        return ref[0, i * n:(i + 1) * n, :]
    return ref[0, pl.ds(i, n, stride=stride), :]


def _inproj_kernel(*refs, r, kinds, rotary, with_f):
    it = iter(refs)
    x_ref, ng_ref, gain_ref, bd_ref = (next(it) for _ in range(4))
    w_refs = [next(it) for _ in kinds]
    if rotary:
        c_ref, s_ref, p_ref = next(it), next(it), next(it)
    if with_f:
        wf_ref = next(it)
    out_ref = next(it)
    if with_f:
        ft_ref = next(it)
    h_scr = next(it)
    if rotary:
        tab_scr = next(it)
    if r > 1:
        hs_scr = next(it)

    tl = ROW_TILE // r
    j = pl.program_id(2)

    @pl.when(j == 0)
    def _():
        x = x_ref[0]
        ms = jnp.mean(x * x, axis=-1, keepdims=True)
        h = x * lax.rsqrt(ms + EPS) * ng_ref[...]
        if r == 1:
            h_scr[...] = h.astype(BF16)
        else:
            for c in range(D_MODEL // LANES):
                hs_scr[c] = h[:, c * LANES:(c + 1) * LANES]
        for i in range(r):
            rows = slice(i * tl, (i + 1) * tl)
            if r > 1:
                for c in range(D_MODEL // LANES):
                    h_scr[rows, c * LANES:(c + 1) * LANES] = hs_scr[
                        c, pl.ds(i, tl, stride=r), :].astype(BF16)
            if rotary:
                tab_scr[0, rows, :] = _strided_rows(c_ref, i, tl, r)
                tab_scr[1, rows, :] = _strided_rows(s_ref, i, tl, r)
        if with_f:
            f = jnp.dot(h_scr[...], wf_ref[...], preferred_element_type=F32)
            ft_ref[0] = f.T

    reps = MXU_DIM // LANES

    for idx, kind in enumerate(kinds):

        @pl.when(j == idx)
        def _(idx=idx, kind=kind):
            nchunk = INNER // MXU_DIM

            def main(cc):
                return jnp.dot(h_scr[...], w_refs[idx][:, cc * MXU_DIM:(cc + 1) * MXU_DIM],
                               preferred_element_type=F32)

            nxt = main(0)
            for cc in range(nchunk):
                cols = slice(cc * MXU_DIM, (cc + 1) * MXU_DIM)
                a = nxt
                if cc + 1 < nchunk:
                    nxt = main(cc + 1)
                if kind == "norm":
                    ms = jnp.dot((a * a).astype(BF16), bd_ref[...], preferred_element_type=F32)
                    rs = lax.rsqrt(ms + EPS)
                    if rotary:
                        u = a * gain_ref[idx, :, cols]
                        up = jnp.dot(u.astype(BF16), p_ref[...], preferred_element_type=F32)
                        cosv = jnp.concatenate([tab_scr[0]] * reps, axis=1)
                        sinv = jnp.concatenate([tab_scr[1]] * reps, axis=1)
                        a = (u * cosv + up * sinv) * rs
                    else:
                        a = a * rs * gain_ref[idx, :, cols]
                ab = a.astype(BF16)
                for i in range(r):
                    for u in range(reps):
                        out_ref[0, i, cc * reps + u] = ab[i * tl:(i + 1) * tl,
                                                          u * LANES:(u + 1) * LANES]


def _head_mean_matrix():
    i = np.arange(MXU_DIM)
    return jnp.asarray((i[:, None] // HEAD_DIM == i[None, :] // HEAD_DIM) / HEAD_DIM, BF16)


def _rotary_partner_matrix():
    half = ROT_DIM // 2
    i = np.arange(MXU_DIM)
    d = i % HEAD_DIM
    partner = np.where(d < half, i + half, np.where(d < ROT_DIM, i - half, -1))
    return jnp.asarray(i[:, None] == partner[None, :], BF16)


def _inproj(x, norm_g, w, sec_idx, gains, kinds, *, r, tabs=None, w_f=None):
    B, S, _ = x.shape
    L = S // r
    tl = ROW_TILE // r
    nsec = len(kinds)
    rotary = tabs is not None
    with_f = w_f is not None

    row_map = lambda b, t, j: (b, t, 0)
    const2 = lambda b, t, j: (0, 0)
    const3 = lambda b, t, j: (0, 0, 0)
    in_specs = [
        pl.BlockSpec((1, ROW_TILE, D_MODEL), row_map),
        pl.BlockSpec((1, D_MODEL), const2),
        pl.BlockSpec((nsec, 1, INNER), const3),
        pl.BlockSpec((MXU_DIM, MXU_DIM), const2),
    ]
    args = [x, norm_g.reshape(1, D_MODEL), gains, _head_mean_matrix()]
    for s in sec_idx:
        in_specs.append(pl.BlockSpec((D_MODEL, INNER), lambda b, t, j, s=s: (0, s)))
        args.append(w)
    scratch = [pltpu.VMEM((ROW_TILE, D_MODEL), BF16)]
    if rotary:
        for t in tabs:
            in_specs.append(pl.BlockSpec((1, ROW_TILE, LANES), row_map))
            args.append(t)
        in_specs.append(pl.BlockSpec((MXU_DIM, MXU_DIM), const2))
        args.append(_rotary_partner_matrix())
        scratch.append(pltpu.VMEM((2, ROW_TILE, LANES), F32))
    if r > 1:
        scratch.append(pltpu.VMEM((D_MODEL // LANES, ROW_TILE, LANES), F32))
    if with_f:
        in_specs.append(pl.BlockSpec((D_MODEL, LANES), const2))
        args.append(w_f)
    out_shape = [jax.ShapeDtypeStruct((B, r, nsec * N_TILES, L, LANES), BF16)]
    out_specs = [pl.BlockSpec((1, r, N_TILES, tl, LANES), lambda b, t, j: (b, 0, j, t, 0))]
    if with_f:
        out_shape.append(jax.ShapeDtypeStruct((B, LANES, S), F32))
        out_specs.append(pl.BlockSpec((1, LANES, ROW_TILE), lambda b, t, j: (b, 0, t)))
    return pl.pallas_call(
        functools.partial(_inproj_kernel, r=r, kinds=kinds, rotary=rotary, with_f=with_f),
        grid=(B, S // ROW_TILE, nsec),
        in_specs=in_specs,
        out_specs=out_specs,
        out_shape=out_shape,
        scratch_shapes=scratch,
        compiler_params=pltpu.CompilerParams(
            dimension_semantics=("arbitrary",) * 3, vmem_limit_bytes=VMEM_LIMIT),
        name=f"inproj_r{r}",
    )(*args)


def _fox_c_kernel(ft_ref, bf_ref, c_ref, *, blk):
    x = ft_ref[0] + bf_ref[...]
    ls = jnp.minimum(x, 0.0) - jnp.log1p(jnp.exp(-jnp.abs(x)))
    S = ls.shape[-1]
    tri = (lax.broadcasted_iota(jnp.int32, (blk, blk), 0)
           <= lax.broadcasted_iota(jnp.int32, (blk, blk), 1)).astype(BF16)
    carry = jnp.zeros((N_HEADS, 1), F32)
    for n in range(S // blk):
        t = ls[:, n * blk:(n + 1) * blk]
        h1 = t.astype(BF16)
        r1 = t - h1.astype(F32)
        h2 = r1.astype(BF16)
        h3 = (r1 - h2.astype(F32)).astype(BF16)
        cs = (jnp.dot(h1, tri, preferred_element_type=F32)
              + jnp.dot(h2, tri, preferred_element_type=F32)
              + jnp.dot(h3, tri, preferred_element_type=F32)) + carry
        c_ref[0, :, n * blk:(n + 1) * blk] = cs
        carry = cs[:, blk - 1:blk]


def _fox_c(ft, b_f):
    B, _, S = ft.shape
    return pl.pallas_call(
        functools.partial(_fox_c_kernel, blk=FOX_SUB),
        grid=(B,),
        in_specs=[pl.BlockSpec((1, N_HEADS, S), lambda b: (b, 0, 0)),
                  pl.BlockSpec((N_HEADS, 1), lambda b: (0, 0))],
        out_specs=pl.BlockSpec((1, N_HEADS, S), lambda b: (b, 0, 0)),
        out_shape=jax.ShapeDtypeStruct((B, N_HEADS, S), F32),
        compiler_params=pltpu.CompilerParams(dimension_semantics=("arbitrary",)),
        name="fox_c",
    )(ft, b_f.reshape(N_HEADS, 1))


def _silu(g):
    return g * jax.nn.sigmoid(g)


def _fox_attn_kernel(q_ref, k_ref, v_ref, g_ref, c_ref, o_ref, m_scr, acc_scr, va_scr,
                     *, tq, sub):
    hp = pl.program_id(1)
    qi = pl.program_id(2)
    S = k_ref.shape[3]
    nsub = S // sub
    per_tile = tq // sub
    low = _lane_is_low((tq, LANES))

    @pl.when(qi == 0)
    def _():
        for n in range(S // tq):
            rows = slice(n * tq, (n + 1) * tq)
            vf = v_ref[0, 0, 0, rows, :].astype(F32)
            va_scr[0, rows, :] = jnp.where(low, vf, 1.0).astype(BF16)
            va_scr[1, rows, :] = jnp.where(low, 1.0, vf).astype(BF16)

    qf = q_ref[0, 0, 0].astype(F32)
    qm = (jnp.where(low, qf, 0.0).astype(BF16), jnp.where(low, 0.0, qf).astype(BF16))
    m_scr[...] = jnp.full(m_scr.shape, NEG, F32)
    acc_scr[...] = jnp.zeros(acc_scr.shape, F32)

    def bias_row(hh, blk0, nblk):
        crow = (hp * HEADS_PER_TILE + hh) * nsub
        c0 = c_ref[0, crow + qi * per_tile][:, 0:1]
        cs = jnp.concatenate([c_ref[0, crow + blk0 + n] for n in range(nblk)], axis=1)
        return (c0 - cs) * LOG2E

    def scores(hh, r0, nr, k0, blk0, nblk, mask):
        kt = k_ref[0, 0, 0, pl.ds(k0, nblk * sub), :]
        s = lax.dot_general(qm[hh][r0:r0 + nr], kt, (((1,), (1,)), ((), ())),
                            preferred_element_type=F32)
        s = s + bias_row(hh, blk0, nblk)
        if mask is not None:
            s = jnp.where(mask, s, NEG)
        return s

    def accumulate(s, hh, r0, nr, k0, blk0, nblk, mask):
        rows = slice(r0, r0 + nr)
        w = nblk * sub
        m_prev = m_scr[hh, rows, :]
        m_new = jnp.maximum(m_prev, jnp.max(s, axis=-1, keepdims=True))
        alpha = jnp.exp2(m_prev - m_new)
        p = jnp.exp2(s - jnp.concatenate([m_new] * (w // LANES), axis=1))
        acc_scr[hh, rows, :] = alpha * acc_scr[hh, rows, :] + jnp.dot(
            p.astype(BF16), va_scr[hh, pl.ds(k0, w), :], preferred_element_type=F32)
        m_scr[hh, rows, :] = m_new

    def step(tiles):
        ss = [scores(*t) for t in tiles]
        for s, t in zip(ss, tiles):
            accumulate(s, *t)

    def body(j, carry):
        k0 = pl.multiple_of(j * tq, tq)
        step([(hh, 0, tq, k0, j * per_tile, per_tile, None) for hh in range(HEADS_PER_TILE)])
        return carry

    lax.fori_loop(0, qi, body, 0)

    k0 = pl.multiple_of(qi * tq, tq)
    k1 = pl.multiple_of(k0 + sub, sub)
    mask0 = (lax.broadcasted_iota(jnp.int32, (tq, sub), 1)
             <= lax.broadcasted_iota(jnp.int32, (tq, sub), 0))
    mask1 = mask0[:sub, :]
    step([(hh, 0, tq, k0, qi * per_tile, 1, mask0) for hh in range(HEADS_PER_TILE)]
         + [(hh, sub, tq - sub, k1, qi * per_tile + 1, 1, mask1) for hh in range(HEADS_PER_TILE)])

    outs = []
    for hh in range(HEADS_PER_TILE):
        acc = acc_scr[hh]
        outs.append(acc / pltpu.roll(acc, HEAD_DIM, 1))
    o = jnp.where(low, outs[0], outs[1])
    o_ref[0, 0] = (o * _silu(g_ref[0, 0, 0].astype(F32))).astype(BF16)


def _fox_attn(qkvg, c):
    B, _, _, S, _ = qkvg.shape
    tq, sub = FOX_TQ, FOX_SUB
    assert tq == 2 * sub
    nrow = N_HEADS * (S // sub)
    c2 = c.reshape(B, nrow, 1, sub)
    return pl.pallas_call(
        functools.partial(_fox_attn_kernel, tq=tq, sub=sub),
        grid=(B, N_TILES, S // tq),
        in_specs=[
            pl.BlockSpec((1, 1, 1, tq, LANES), lambda b, hp, qi: (b, 0, hp, qi, 0)),
            pl.BlockSpec((1, 1, 1, S, LANES), lambda b, hp, qi: (b, 0, N_TILES + hp, 0, 0)),
            pl.BlockSpec((1, 1, 1, S, LANES), lambda b, hp, qi: (b, 0, 2 * N_TILES + hp, 0, 0)),
            pl.BlockSpec((1, 1, 1, tq, LANES), lambda b, hp, qi: (b, 0, 3 * N_TILES + hp, qi, 0)),
            pl.BlockSpec((1, nrow, 1, sub), lambda b, hp, qi: (b, 0, 0, 0)),
        ],
        out_specs=pl.BlockSpec((1, 1, tq, LANES), lambda b, hp, qi: (b, hp, qi, 0)),
        out_shape=jax.ShapeDtypeStruct((B, N_TILES, S, LANES), BF16),
        scratch_shapes=[pltpu.VMEM((HEADS_PER_TILE, tq, LANES), F32),
                        pltpu.VMEM((HEADS_PER_TILE, tq, LANES), F32),
                        pltpu.VMEM((HEADS_PER_TILE, S, LANES), BF16)],
        compiler_params=pltpu.CompilerParams(
            dimension_semantics=("arbitrary",) * 3, vmem_limit_bytes=VMEM_LIMIT),
        name="fox_attn",
    )(qkvg, qkvg, qkvg, qkvg, c2)


def _dsw_attn_kernel(*refs):
    (q0, k0, v0, gate_ref, q1, k1, v1, q2, k2, v2, o_ref, og_scr, lse_scr, *vx_scr) = refs
    groups = ((q0, k0, v0), (q1, k1, v1), (q2, k2, v2))
    S = o_ref.shape[2]
    low = _lane_is_low((BLOCK, LANES))
    row = lax.broadcasted_iota(jnp.int32, (BLOCK, 2 * BLOCK), 0)
    col = lax.broadcasted_iota(jnp.int32, (BLOCK, 2 * BLOCK), 1)
    band = (col >= row) & (col <= row + BLOCK)
    band0 = band & (col >= BLOCK)
    band2 = jnp.concatenate([band] * HEADS_PER_TILE, axis=0)
    band02 = jnp.concatenate([band0] * HEADS_PER_TILE, axis=0)

    def band_slices(n):
        cur = slice(n * BLOCK, (n + 1) * BLOCK)
        prev = slice(max(n - 1, 0) * BLOCK, (max(n - 1, 0) + 1) * BLOCK)
        return prev, cur

    one_lo = jnp.where(low, 1.0, 0.0).astype(BF16)
    one_hi = jnp.where(low, 0.0, 1.0).astype(BF16)
    for g, (_, r) in enumerate(DSWA_GROUPS):
        v_ref = groups[g][2]
        for z in range(r):
            for n in range(S // r // BLOCK):
                cur = slice(n * BLOCK, (n + 1) * BLOCK)
                vf = v_ref[0, z, 0, cur, :].astype(F32)
                vx_scr[g][0, z, cur, :LANES] = jnp.where(low, vf, 0.0).astype(BF16)
                vx_scr[g][0, z, cur, LANES:] = one_lo
                vx_scr[g][1, z, cur, :LANES] = jnp.where(low, 0.0, vf).astype(BF16)
                vx_scr[g][1, z, cur, LANES:] = one_hi

    def block(g, z, n):
        q_ref, k_ref, _ = groups[g]
        r = DSWA_GROUPS[g][1]
        prev, cur = band_slices(n)
        qf = q_ref[0, z, 0, cur, :].astype(F32)
        kb = jnp.concatenate([k_ref[0, z, 0, prev, :], k_ref[0, z, 0, cur, :]], axis=0)
        valid = band2 if n > 0 else band02
        qm = jnp.concatenate([jnp.where(low, qf, 0.0), jnp.where(low, 0.0, qf)],
                             axis=0).astype(BF16)
        s = lax.dot_general(qm, kb, (((1,), (1,)), ((), ())), preferred_element_type=F32)
        s = jnp.where(valid, s, NEG)
        m = jnp.max(s, axis=-1, keepdims=True)
        p = jnp.exp2(s - m).astype(BF16)
        vx = jnp.concatenate([vx_scr[g][hh, z, sl, :] for hh in range(HEADS_PER_TILE)
                              for sl in (prev, cur)], axis=0)
        ol = jnp.dot(jnp.concatenate([p[:BLOCK], p[BLOCK:]], axis=1), vx,
                     preferred_element_type=F32)
        l = ol[:, LANES:]
        dst = pl.ds(z + r * n * BLOCK, BLOCK, stride=r) if r > 1 else cur
        og_scr[g, dst, :] = ol[:, :LANES] / l
        lse_scr[g, dst, :] = jnp.where(low, m[:BLOCK], m[BLOCK:]) + jnp.log2(l)

    for g, (_, r) in enumerate(DSWA_GROUPS):
        for z in range(r):
            for n in range(S // r // BLOCK):
                block(g, z, n)

    chunk = 2 * BLOCK

    def combine(i, carry):
        rows = pl.ds(pl.multiple_of(i * chunk, chunk), chunk)
        l0, l1, l2 = lse_scr[0, rows, :], lse_scr[1, rows, :], lse_scr[2, rows, :]
        mx = jnp.maximum(jnp.maximum(l0, l1), l2)
        e0, e1, e2 = jnp.exp2(l0 - mx), jnp.exp2(l1 - mx), jnp.exp2(l2 - mx)
        o = (e0 * og_scr[0, rows, :] + e1 * og_scr[1, rows, :] + e2 * og_scr[2, rows, :]) / (
            e0 + e1 + e2)
        o_ref[0, 0, rows, :] = (o * _silu(gate_ref[0, 0, 0, rows, :].astype(F32))).astype(BF16)
        return carry

    lax.fori_loop(0, S // chunk, combine, 0)


def _dsw_attn(p0, p1, p2):
    B, _, _, S, _ = p0.shape
    in_specs, args = [], []
    for g, (p, (_, r)) in enumerate(zip((p0, p1, p2), DSWA_GROUPS)):
        for sec in range(3):
            in_specs.append(pl.BlockSpec(
                (1, r, 1, S // r, LANES),
                lambda b, hp, sec=sec: (b, 0, sec * N_TILES + hp, 0, 0)))
            args.append(p)
        if g == 0:
            in_specs.append(pl.BlockSpec(
                (1, 1, 1, S, LANES), lambda b, hp: (b, 0, 3 * N_TILES + hp, 0, 0)))
            args.append(p)
    return pl.pallas_call(
        _dsw_attn_kernel,
        grid=(B, N_TILES),
        in_specs=in_specs,
        out_specs=pl.BlockSpec((1, 1, S, LANES), lambda b, hp: (b, hp, 0, 0)),
        out_shape=jax.ShapeDtypeStruct((B, N_TILES, S, LANES), BF16),
        scratch_shapes=[pltpu.VMEM((3, S, LANES), F32), pltpu.VMEM((3, S, LANES), F32)] + [
            pltpu.VMEM((HEADS_PER_TILE, r, S // r, 2 * LANES), BF16) for _, r in DSWA_GROUPS],
        compiler_params=pltpu.CompilerParams(
            dimension_semantics=("arbitrary",) * 2, vmem_limit_bytes=VMEM_LIMIT),
        name="dsw_attn",
    )(*args)


def _outproj_kernel(a_ref, w_ref, x_ref, o_ref):
    a = jnp.concatenate([a_ref[0, c] for c in range(N_TILES)], axis=1)
    o_ref[0] = x_ref[0] + jnp.dot(a, w_ref[...], preferred_element_type=F32)


def _outproj(a, w, x):
    B, S, _ = x.shape
    return pl.pallas_call(
        _outproj_kernel,
        grid=(B, S // ROW_TILE),
        in_specs=[pl.BlockSpec((1, N_TILES, ROW_TILE, LANES), lambda b, t: (b, 0, t, 0)),
                  pl.BlockSpec((INNER, D_MODEL), lambda b, t: (0, 0)),
                  pl.BlockSpec((1, ROW_TILE, D_MODEL), lambda b, t: (b, t, 0))],
        out_specs=pl.BlockSpec((1, ROW_TILE, D_MODEL), lambda b, t: (b, t, 0)),
        out_shape=jax.ShapeDtypeStruct((B, S, D_MODEL), F32),
        compiler_params=pltpu.CompilerParams(
            dimension_semantics=("arbitrary",) * 2, vmem_limit_bytes=VMEM_LIMIT),
        name="outproj",
    )(a, w, x)


def _gain_rows(q_g, k_g, n_plain):
    rows = [jnp.tile(q_g * (SCALE * LOG2E), N_HEADS), jnp.tile(k_g, N_HEADS)]
    rows += [jnp.ones((INNER,), F32)] * n_plain
    return jnp.stack(rows, axis=0).reshape(len(rows), 1, INNER).astype(F32)


def _rotary_tables(positions):
    half = ROT_DIM // 2
    inv_freq = ROPE_THETA ** (-jnp.arange(0, ROT_DIM, 2, dtype=F32) / ROT_DIM)
    ang = positions.astype(F32)[..., None] * inv_freq
    cos, sin = jnp.cos(ang), jnp.sin(ang)
    B, S = positions.shape
    one = jnp.ones((B, S, HEAD_DIM - ROT_DIM), F32)
    zero = jnp.zeros((B, S, HEAD_DIM - ROT_DIM), F32)
    c = jnp.concatenate([cos, cos, one], axis=-1)
    s = jnp.concatenate([-sin, sin, zero], axis=-1)
    return tuple(jnp.tile(t, (1, 1, HEADS_PER_TILE)) for t in (c, s))


def kernel(x, positions, norm_g, fox_w_in, fox_b_f, fox_q_norm, fox_k_norm, fox_w_out,
           dsw_w_in, dsw_q_norm, dsw_k_norm, dsw_w_out):
    B, S, _ = x.shape
    depth = norm_g.shape[0]
    tabs = _rotary_tables(positions)
    for i in range(depth):
        j = i // 2
        if i % 2 == 0:
            w = fox_w_in[j].astype(BF16)
            w_f = jnp.pad(w[:, 4 * INNER:], ((0, 0), (0, LANES - N_HEADS)))
            gains = _gain_rows(fox_q_norm[j], fox_k_norm[j], 2)
            qkvg, ft = _inproj(x, norm_g[i], w, (0, 1, 2, 3), gains,
                               ("norm", "norm", "plain", "plain"), r=1, w_f=w_f)
            c = _fox_c(ft, fox_b_f[j])
            a = _fox_attn(qkvg, c)
            w_out = fox_w_out[j]
        else:
            projs = []
            w = dsw_w_in[j].astype(BF16)
            n_groups = len(DSWA_GROUPS)
            for g, (_, r) in enumerate(DSWA_GROUPS):
                idxs = (g, n_groups + g, 2 * n_groups + g) + ((3 * n_groups,) if g == 0 else ())
                kinds = ("norm", "norm", "plain") + (("plain",) if g == 0 else ())
                gains = _gain_rows(dsw_q_norm[j, g], dsw_k_norm[j, g], len(kinds) - 2)
                projs.append(_inproj(x, norm_g[i], w, idxs, gains, kinds, r=r, tabs=tabs)[0])
            a = _dsw_attn(*projs)
            w_out = dsw_w_out[j]
        x = _outproj(a, w_out.astype(BF16), x)
    return x
```

```python
import functools

import jax
import jax.numpy as jnp
import numpy as np
from jax import lax
from jax.experimental import pallas as pl
from jax.experimental.pallas import tpu as pltpu

D_MODEL = 1024
HEAD_DIM = 64
N_HEADS = 16
INNER = N_HEADS * HEAD_DIM
ROT_DIM = HEAD_DIM // 4
ROPE_THETA = 500000.0
DSWA_GROUPS = ((128, 1), (512, 4), (2048, 16))
BLOCK = 128
EPS = 1e-6
SCALE = HEAD_DIM ** -0.5
LOG2E = 1.4426950408889634

LANES = 128
MXU_DIM = 256
HEADS_PER_TILE = LANES // HEAD_DIM
N_TILES = INNER // LANES
NEG = -1e30
VMEM_LIMIT = 48 * 1024 * 1024

ROW_TILE = 1024
OUT_ROW_TILE = 512
FOX_TQ = 512
FOX_SUB = 256

F32 = jnp.float32
BF16 = jnp.bfloat16


def _lane_is_low(shape):
    return lax.broadcasted_iota(jnp.int32, shape, len(shape) - 1) < HEAD_DIM


def _strided_rows(ref, i, n, stride):
    if stride == 1:
        return ref[0, i * n:(i + 1) * n, :]
    return ref[0, pl.ds(i, n, stride=stride), :]


def _inproj_kernel(*refs, r, kinds, rotary, with_f):
    it = iter(refs)
    x_ref, ng_ref, gain_ref, bd_ref = (next(it) for _ in range(4))
    w_refs = [next(it) for _ in kinds]
    if rotary:
        c_ref, s_ref, p_ref = next(it), next(it), next(it)
    if with_f:
        wf_ref = next(it)
    out_ref = next(it)
    if with_f:
        ft_ref = next(it)
    h_scr = next(it)
    if rotary:
        tab_scr = next(it)
    if r > 1:
        hs_scr = next(it)

    tl = ROW_TILE // r
    j = pl.program_id(2)

    @pl.when(j == 0)
    def _():
        x = x_ref[0]
        ms = jnp.mean(x * x, axis=-1, keepdims=True)
        h = x * lax.rsqrt(ms + EPS) * ng_ref[...]
        if r == 1:
            h_scr[...] = h.astype(BF16)
        else:
            for c in range(D_MODEL // LANES):
                hs_scr[c] = h[:, c * LANES:(c + 1) * LANES]
        for i in range(r):
            rows = slice(i * tl, (i + 1) * tl)
            if r > 1:
                for c in range(D_MODEL // LANES):
                    h_scr[rows, c * LANES:(c + 1) * LANES] = hs_scr[
                        c, pl.ds(i, tl, stride=r), :].astype(BF16)
            if rotary:
                tab_scr[0, rows, :] = _strided_rows(c_ref, i, tl, r)
                tab_scr[1, rows, :] = _strided_rows(s_ref, i, tl, r)
        if with_f:
            f = jnp.dot(h_scr[...], wf_ref[...], preferred_element_type=F32)
            ft_ref[0] = f.T

    reps = MXU_DIM // LANES

    for idx, kind in enumerate(kinds):

        @pl.when(j == idx)
        def _(idx=idx, kind=kind):
            nchunk = INNER // MXU_DIM

            def main(cc):
                return jnp.dot(h_scr[...], w_refs[idx][:, cc * MXU_DIM:(cc + 1) * MXU_DIM],
                               preferred_element_type=F32)

            nxt = main(0)
            for cc in range(nchunk):
                cols = slice(cc * MXU_DIM, (cc + 1) * MXU_DIM)
                a = nxt
                if cc + 1 < nchunk:
                    nxt = main(cc + 1)
                if kind == "norm":
                    ms = jnp.dot((a * a).astype(BF16), bd_ref[...], preferred_element_type=F32)
                    rs = lax.rsqrt(ms + EPS)
                    if rotary:
                        u = a * gain_ref[idx, :, cols]
                        up = jnp.dot(u.astype(BF16), p_ref[...], preferred_element_type=F32)
                        cosv = jnp.concatenate([tab_scr[0]] * reps, axis=1)
                        sinv = jnp.concatenate([tab_scr[1]] * reps, axis=1)
                        a = (u * cosv + up * sinv) * rs
                    else:
                        a = a * rs * gain_ref[idx, :, cols]
                ab = a.astype(BF16)
                for i in range(r):
                    for u in range(reps):
                        out_ref[0, i, cc * reps + u] = ab[i * tl:(i + 1) * tl,
                                                          u * LANES:(u + 1) * LANES]


def _head_mean_matrix():
    i = np.arange(MXU_DIM)
    return jnp.asarray((i[:, None] // HEAD_DIM == i[None, :] // HEAD_DIM) / HEAD_DIM, BF16)


def _rotary_partner_matrix():
    half = ROT_DIM // 2
    i = np.arange(MXU_DIM)
    d = i % HEAD_DIM
    partner = np.where(d < half, i + half, np.where(d < ROT_DIM, i - half, -1))
    return jnp.asarray(i[:, None] == partner[None, :], BF16)


def _inproj(x, norm_g, w, sec_idx, gains, kinds, *, r, tabs=None, w_f=None):
    B, S, _ = x.shape
    L = S // r
    tl = ROW_TILE // r
    nsec = len(kinds)
    rotary = tabs is not None
    with_f = w_f is not None

    row_map = lambda b, t, j: (b, t, 0)
    const2 = lambda b, t, j: (0, 0)
    const3 = lambda b, t, j: (0, 0, 0)
    in_specs = [
        pl.BlockSpec((1, ROW_TILE, D_MODEL), row_map),
        pl.BlockSpec((1, D_MODEL), const2),
        pl.BlockSpec((nsec, 1, INNER), const3),
        pl.BlockSpec((MXU_DIM, MXU_DIM), const2),
    ]
    args = [x, norm_g.reshape(1, D_MODEL), gains, _head_mean_matrix()]
    for s in sec_idx:
        in_specs.append(pl.BlockSpec((D_MODEL, INNER), lambda b, t, j, s=s: (0, s)))
        args.append(w)
    scratch = [pltpu.VMEM((ROW_TILE, D_MODEL), BF16)]
    if rotary:
        for t in tabs:
            in_specs.append(pl.BlockSpec((1, ROW_TILE, LANES), row_map))
            args.append(t)
        in_specs.append(pl.BlockSpec((MXU_DIM, MXU_DIM), const2))
        args.append(_rotary_partner_matrix())
        scratch.append(pltpu.VMEM((2, ROW_TILE, LANES), F32))
    if r > 1:
        scratch.append(pltpu.VMEM((D_MODEL // LANES, ROW_TILE, LANES), F32))
    if with_f:
        in_specs.append(pl.BlockSpec((D_MODEL, LANES), const2))
        args.append(w_f)
    out_shape = [jax.ShapeDtypeStruct((B, r, nsec * N_TILES, L, LANES), BF16)]
    out_specs = [pl.BlockSpec((1, r, N_TILES, tl, LANES), lambda b, t, j: (b, 0, j, t, 0))]
    if with_f:
        out_shape.append(jax.ShapeDtypeStruct((B, LANES, S), F32))
        out_specs.append(pl.BlockSpec((1, LANES, ROW_TILE), lambda b, t, j: (b, 0, t)))
    return pl.pallas_call(
        functools.partial(_inproj_kernel, r=r, kinds=kinds, rotary=rotary, with_f=with_f),
        grid=(B, S // ROW_TILE, nsec),
        in_specs=in_specs,
        out_specs=out_specs,
        out_shape=out_shape,
        scratch_shapes=scratch,
        compiler_params=pltpu.CompilerParams(
            dimension_semantics=("arbitrary",) * 3, vmem_limit_bytes=VMEM_LIMIT),
        name=f"inproj_r{r}",
    )(*args)


def _fox_c_kernel(ft_ref, bf_ref, c_ref, *, blk):
    x = ft_ref[0] + bf_ref[...]
    ls = jnp.minimum(x, 0.0) - jnp.log1p(jnp.exp(-jnp.abs(x)))
    S = ls.shape[-1]
    tri = (lax.broadcasted_iota(jnp.int32, (blk, blk), 0)
           <= lax.broadcasted_iota(jnp.int32, (blk, blk), 1)).astype(BF16)
    carry = jnp.zeros((N_HEADS, 1), F32)
    for n in range(S // blk):
        t = ls[:, n * blk:(n + 1) * blk]
        h1 = t.astype(BF16)
        r1 = t - h1.astype(F32)
        h2 = r1.astype(BF16)
        h3 = (r1 - h2.astype(F32)).astype(BF16)
        cs = (jnp.dot(h1, tri, preferred_element_type=F32)
              + jnp.dot(h2, tri, preferred_element_type=F32)
              + jnp.dot(h3, tri, preferred_element_type=F32)) + carry
        c_ref[0, :, n * blk:(n + 1) * blk] = cs
        carry = cs[:, blk - 1:blk]


def _fox_c(ft, b_f):
    B, _, S = ft.shape
    return pl.pallas_call(
        functools.partial(_fox_c_kernel, blk=FOX_SUB),
        grid=(B,),
        in_specs=[pl.BlockSpec((1, N_HEADS, S), lambda b: (b, 0, 0)),
                  pl.BlockSpec((N_HEADS, 1), lambda b: (0, 0))],
        out_specs=pl.BlockSpec((1, N_HEADS, S), lambda b: (b, 0, 0)),
        out_shape=jax.ShapeDtypeStruct((B, N_HEADS, S), F32),
        compiler_params=pltpu.CompilerParams(dimension_semantics=("arbitrary",)),
        name="fox_c",
    )(ft, b_f.reshape(N_HEADS, 1))


def _silu(g):
    return g * jax.nn.sigmoid(g)


def _fox_attn_kernel(q_ref, k_ref, v_ref, g_ref, c_ref, o_ref, m_scr, acc_scr, va_scr,
                     *, tq, sub):
    hp = pl.program_id(1)
    qi = pl.program_id(2)
    S = k_ref.shape[3]
    nsub = S // sub
    per_tile = tq // sub
    low = _lane_is_low((tq, LANES))

    @pl.when(qi == 0)
    def _():
        for n in range(S // tq):
            rows = slice(n * tq, (n + 1) * tq)
            vf = v_ref[0, 0, 0, rows, :].astype(F32)
            va_scr[0, rows, :] = jnp.where(low, vf, 1.0).astype(BF16)
            va_scr[1, rows, :] = jnp.where(low, 1.0, vf).astype(BF16)

    qf = q_ref[0, 0, 0].astype(F32)
    qm = (jnp.where(low, qf, 0.0).astype(BF16), jnp.where(low, 0.0, qf).astype(BF16))
    m_scr[...] = jnp.full(m_scr.shape, NEG, F32)
    acc_scr[...] = jnp.zeros(acc_scr.shape, F32)

    mask = (lax.broadcasted_iota(jnp.int32, (tq, sub), 1)
            <= lax.broadcasted_iota(jnp.int32, (tq, sub), 0))

    def run(nq):
        def bias_row(hh, blk0, nblk):
            crow = (hp * HEADS_PER_TILE + hh) * nsub
            c0 = c_ref[0, crow + nq * per_tile][:, 0:1]
            cs = jnp.concatenate([c_ref[0, crow + blk0 + n] for n in range(nblk)], axis=1)
            return (c0 - cs) * LOG2E

        def scores(r0, blk0, nblk):
            kt = k_ref[0, 0, 0, blk0 * sub:(blk0 + nblk) * sub, :]
            return tuple(
                lax.dot_general(qm[hh][r0:], kt, (((1,), (1,)), ((), ())),
                                preferred_element_type=F32) + bias_row(hh, blk0, nblk)
                for hh in range(HEADS_PER_TILE))

        def accumulate(s, hh, r0, nr, k0, w):
            rows = slice(r0, r0 + nr)
            m_prev = m_scr[hh, rows, :]
            m_new = jnp.maximum(m_prev, jnp.max(s, axis=-1, keepdims=True))
            alpha = jnp.exp2(m_prev - m_new)
            p = jnp.exp2(s - jnp.concatenate([m_new] * (w // LANES), axis=1))
            acc_scr[hh, rows, :] = alpha * acc_scr[hh, rows, :] + jnp.dot(
                p.astype(BF16), va_scr[hh, k0:k0 + w, :], preferred_element_type=F32)
            m_scr[hh, rows, :] = m_new

        steps = [(0, j * per_tile, per_tile, None) for j in range(nq)]
        steps += [(0, nq * per_tile, 1, mask), (sub, nq * per_tile + 1, 1, mask[:sub])]
        s_cur = scores(*steps[0][:3])
        for i, (r0, blk0, nblk, msk) in enumerate(steps):
            if i + 1 < len(steps):
                s_next = scores(*steps[i + 1][:3])
            for hh in range(HEADS_PER_TILE):
                s = s_cur[hh] if msk is None else jnp.where(msk, s_cur[hh], NEG)
                accumulate(s, hh, r0, tq - r0, blk0 * sub, nblk * sub)
            s_cur = s_next

    for nq in range(S // tq):
        pl.when(qi == nq)(functools.partial(run, nq))

    num = jnp.where(low, acc_scr[0], acc_scr[1])
    den = pltpu.roll(jnp.where(low, acc_scr[1], acc_scr[0]), HEAD_DIM, 1)
    o_ref[0, 0] = (num / den * _silu(g_ref[0, 0, 0].astype(F32))).astype(BF16)


def _fox_attn(qkvg, c):
    B, _, _, S, _ = qkvg.shape
    tq, sub = FOX_TQ, FOX_SUB
    assert tq == 2 * sub
    nrow = N_HEADS * (S // sub)
    c2 = c.reshape(B, nrow, 1, sub)
    return pl.pallas_call(
        functools.partial(_fox_attn_kernel, tq=tq, sub=sub),
        grid=(B, N_TILES, S // tq),
        in_specs=[
            pl.BlockSpec((1, 1, 1, tq, LANES), lambda b, hp, qi: (b, 0, hp, qi, 0)),
            pl.BlockSpec((1, 1, 1, S, LANES), lambda b, hp, qi: (b, 0, N_TILES + hp, 0, 0)),
            pl.BlockSpec((1, 1, 1, S, LANES), lambda b, hp, qi: (b, 0, 2 * N_TILES + hp, 0, 0)),
            pl.BlockSpec((1, 1, 1, tq, LANES), lambda b, hp, qi: (b, 0, 3 * N_TILES + hp, qi, 0)),
            pl.BlockSpec((1, nrow, 1, sub), lambda b, hp, qi: (b, 0, 0, 0)),
        ],
        out_specs=pl.BlockSpec((1, 1, tq, LANES), lambda b, hp, qi: (b, hp, qi, 0)),
        out_shape=jax.ShapeDtypeStruct((B, N_TILES, S, LANES), BF16),
        scratch_shapes=[pltpu.VMEM((HEADS_PER_TILE, tq, LANES), F32),
                        pltpu.VMEM((HEADS_PER_TILE, tq, LANES), F32),
                        pltpu.VMEM((HEADS_PER_TILE, S, LANES), BF16)],
        compiler_params=pltpu.CompilerParams(
            dimension_semantics=("arbitrary",) * 3, vmem_limit_bytes=VMEM_LIMIT),
        name="fox_attn",
    )(qkvg, qkvg, qkvg, qkvg, c2)


def _dsw_attn_kernel(*refs):
    (q0, k0, v0, gate_ref, q1, k1, v1, q2, k2, v2, o_ref, og_scr, lse_scr, *vx_scr) = refs
    groups = ((q0, k0, v0), (q1, k1, v1), (q2, k2, v2))
    S = o_ref.shape[2]
    low = _lane_is_low((BLOCK, LANES))
    row = lax.broadcasted_iota(jnp.int32, (BLOCK, 2 * BLOCK), 0)
    col = lax.broadcasted_iota(jnp.int32, (BLOCK, 2 * BLOCK), 1)
    band = (col >= row) & (col <= row + BLOCK)
    band0 = band & (col >= BLOCK)
    band2 = jnp.concatenate([band] * HEADS_PER_TILE, axis=0)
    band02 = jnp.concatenate([band0] * HEADS_PER_TILE, axis=0)

    def band_slices(n):
        cur = slice(n * BLOCK, (n + 1) * BLOCK)
        prev = slice(max(n - 1, 0) * BLOCK, (max(n - 1, 0) + 1) * BLOCK)
        return prev, cur

    one_lo = jnp.where(low, 1.0, 0.0).astype(BF16)
    one_hi = jnp.where(low, 0.0, 1.0).astype(BF16)
    for g, (_, r) in enumerate(DSWA_GROUPS):
        v_ref = groups[g][2]
        for z in range(r):
            for n in range(S // r // BLOCK):
                cur = slice(n * BLOCK, (n + 1) * BLOCK)
                vf = v_ref[0, z, 0, cur, :].astype(F32)
                vx_scr[g][0, z, cur, :LANES] = jnp.where(low, vf, 0.0).astype(BF16)
                vx_scr[g][0, z, cur, LANES:] = one_lo
                vx_scr[g][1, z, cur, :LANES] = jnp.where(low, 0.0, vf).astype(BF16)
                vx_scr[g][1, z, cur, LANES:] = one_hi

    def block(g, z, n):
        q_ref, k_ref, _ = groups[g]
        r = DSWA_GROUPS[g][1]
        prev, cur = band_slices(n)
        qf = q_ref[0, z, 0, cur, :].astype(F32)
        kb = jnp.concatenate([k_ref[0, z, 0, prev, :], k_ref[0, z, 0, cur, :]], axis=0)
        valid = band2 if n > 0 else band02
        qm = jnp.concatenate([jnp.where(low, qf, 0.0), jnp.where(low, 0.0, qf)],
                             axis=0).astype(BF16)
        s = lax.dot_general(qm, kb, (((1,), (1,)), ((), ())), preferred_element_type=F32)
        s = jnp.where(valid, s, NEG)
        m = jnp.max(s, axis=-1, keepdims=True)
        p = jnp.exp2(s - m).astype(BF16)
        vx = jnp.concatenate([vx_scr[g][hh, z, sl, :] for hh in range(HEADS_PER_TILE)
                              for sl in (prev, cur)], axis=0)
        ol = jnp.dot(jnp.concatenate([p[:BLOCK], p[BLOCK:]], axis=1), vx,
                     preferred_element_type=F32)
        l = ol[:, LANES:]
        dst = pl.ds(z + r * n * BLOCK, BLOCK, stride=r) if r > 1 else cur
        og_scr[g, dst, :] = ol[:, :LANES] / l
        lse_scr[g, dst, :] = jnp.where(low, m[:BLOCK], m[BLOCK:]) + jnp.log2(l)

    for g, (_, r) in enumerate(DSWA_GROUPS):
        for z in range(r):
            for n in range(S // r // BLOCK):
                block(g, z, n)

    chunk = 2 * BLOCK

    def combine(i, carry):
        rows = pl.ds(pl.multiple_of(i * chunk, chunk), chunk)
        l0, l1, l2 = lse_scr[0, rows, :], lse_scr[1, rows, :], lse_scr[2, rows, :]
        mx = jnp.maximum(jnp.maximum(l0, l1), l2)
        e0, e1, e2 = jnp.exp2(l0 - mx), jnp.exp2(l1 - mx), jnp.exp2(l2 - mx)
        o = (e0 * og_scr[0, rows, :] + e1 * og_scr[1, rows, :] + e2 * og_scr[2, rows, :]) / (
            e0 + e1 + e2)
        o_ref[0, 0, rows, :] = (o * _silu(gate_ref[0, 0, 0, rows, :].astype(F32))).astype(BF16)
        return carry

    lax.fori_loop(0, S // chunk, combine, 0)


def _dsw_attn(p0, p1, p2):
    B, _, _, S, _ = p0.shape
    in_specs, args = [], []
    for g, (p, (_, r)) in enumerate(zip((p0, p1, p2), DSWA_GROUPS)):
        for sec in range(3):
            in_specs.append(pl.BlockSpec(
                (1, r, 1, S // r, LANES),
                lambda b, hp, sec=sec: (b, 0, sec * N_TILES + hp, 0, 0)))
            args.append(p)
        if g == 0:
            in_specs.append(pl.BlockSpec(
                (1, 1, 1, S, LANES), lambda b, hp: (b, 0, 3 * N_TILES + hp, 0, 0)))
            args.append(p)
    return pl.pallas_call(
        _dsw_attn_kernel,
        grid=(B, N_TILES),
        in_specs=in_specs,
        out_specs=pl.BlockSpec((1, 1, S, LANES), lambda b, hp: (b, hp, 0, 0)),
        out_shape=jax.ShapeDtypeStruct((B, N_TILES, S, LANES), BF16),
        scratch_shapes=[pltpu.VMEM((3, S, LANES), F32), pltpu.VMEM((3, S, LANES), F32)] + [
            pltpu.VMEM((HEADS_PER_TILE, r, S // r, 2 * LANES), BF16) for _, r in DSWA_GROUPS],
        compiler_params=pltpu.CompilerParams(
            dimension_semantics=("arbitrary",) * 2, vmem_limit_bytes=VMEM_LIMIT),
        name="dsw_attn",
    )(*args)


def _outproj_kernel(a_ref, w_ref, x_ref, o_ref):
    a = jnp.concatenate([a_ref[0, c] for c in range(N_TILES)], axis=1)
    o_ref[0] = x_ref[0] + jnp.dot(a, w_ref[...], preferred_element_type=F32)


def _outproj(a, w, x):
    B, S, _ = x.shape
    return pl.pallas_call(
        _outproj_kernel,
        grid=(B, S // OUT_ROW_TILE),
        in_specs=[pl.BlockSpec((1, N_TILES, OUT_ROW_TILE, LANES), lambda b, t: (b, 0, t, 0)),
                  pl.BlockSpec((INNER, D_MODEL), lambda b, t: (0, 0)),
                  pl.BlockSpec((1, OUT_ROW_TILE, D_MODEL), lambda b, t: (b, t, 0))],
        out_specs=pl.BlockSpec((1, OUT_ROW_TILE, D_MODEL), lambda b, t: (b, t, 0)),
        out_shape=jax.ShapeDtypeStruct((B, S, D_MODEL), F32),
        compiler_params=pltpu.CompilerParams(
            dimension_semantics=("arbitrary",) * 2, vmem_limit_bytes=VMEM_LIMIT),
        name="outproj",
    )(a, w, x)


def _gain_rows(q_g, k_g, n_plain):
    rows = [jnp.tile(q_g * (SCALE * LOG2E), N_HEADS), jnp.tile(k_g, N_HEADS)]
    rows += [jnp.ones((INNER,), F32)] * n_plain
    return jnp.stack(rows, axis=0).reshape(len(rows), 1, INNER).astype(F32)


def _rotary_tables(positions):
    half = ROT_DIM // 2
    inv_freq = ROPE_THETA ** (-jnp.arange(0, ROT_DIM, 2, dtype=F32) / ROT_DIM)
    ang = positions.astype(F32)[..., None] * inv_freq
    cos, sin = jnp.cos(ang), jnp.sin(ang)
    B, S = positions.shape
    one = jnp.ones((B, S, HEAD_DIM - ROT_DIM), F32)
    zero = jnp.zeros((B, S, HEAD_DIM - ROT_DIM), F32)
    c = jnp.concatenate([cos, cos, one], axis=-1)
    s = jnp.concatenate([-sin, sin, zero], axis=-1)
    return tuple(jnp.tile(t, (1, 1, HEADS_PER_TILE)) for t in (c, s))


def kernel(x, positions, norm_g, fox_w_in, fox_b_f, fox_q_norm, fox_k_norm, fox_w_out,
           dsw_w_in, dsw_q_norm, dsw_k_norm, dsw_w_out):
    B, S, _ = x.shape
    depth = norm_g.shape[0]
    tabs = _rotary_tables(positions)
    for i in range(depth):
        j = i // 2
        if i % 2 == 0:
            w = fox_w_in[j].astype(BF16)
            w_f = jnp.pad(w[:, 4 * INNER:], ((0, 0), (0, LANES - N_HEADS)))
            gains = _gain_rows(fox_q_norm[j], fox_k_norm[j], 2)
            qkvg, ft = _inproj(x, norm_g[i], w, (0, 1, 2, 3), gains,
                               ("norm", "norm", "plain", "plain"), r=1, w_f=w_f)
            c = _fox_c(ft, fox_b_f[j])
            a = _fox_attn(qkvg, c)
            w_out = fox_w_out[j]
        else:
            projs = []
            w = dsw_w_in[j].astype(BF16)
            n_groups = len(DSWA_GROUPS)
            for g, (_, r) in enumerate(DSWA_GROUPS):
                idxs = (g, n_groups + g, 2 * n_groups + g) + ((3 * n_groups,) if g == 0 else ())
                kinds = ("norm", "norm", "plain") + (("plain",) if g == 0 else ())
                gains = _gain_rows(dsw_q_norm[j, g], dsw_k_norm[j, g], len(kinds) - 2)
                projs.append(_inproj(x, norm_g[i], w, idxs, gains, kinds, r=r, tabs=tabs)[0])
            a = _dsw_attn(*projs)
            w_out = dsw_w_out[j]
        x = _outproj(a, w_out.astype(BF16), x)
    return x
```

```python
import functools

import jax
import jax.numpy as jnp
import numpy as np
from jax import lax
from jax.experimental import pallas as pl
from jax.experimental.pallas import tpu as pltpu

D_MODEL = 1024
HEAD_DIM = 64
N_HEADS = 16
INNER = N_HEADS * HEAD_DIM
ROT_DIM = HEAD_DIM // 4
ROPE_THETA = 500000.0
DSWA_GROUPS = ((128, 1), (512, 4), (2048, 16))
BLOCK = 128
EPS = 1e-6
SCALE = HEAD_DIM ** -0.5
LOG2E = 1.4426950408889634

LANES = 128
MXU_DIM = 256
HEADS_PER_TILE = LANES // HEAD_DIM
N_TILES = INNER // LANES
NEG = -1e30
VMEM_LIMIT = 58 * 1024 * 1024

ROW_TILE = 1024
OUT_ROW_TILE = 512
FOX_TQ = 512
FOX_SUB = 256

F32 = jnp.float32
BF16 = jnp.bfloat16


def _lane_is_low(shape):
    return lax.broadcasted_iota(jnp.int32, shape, len(shape) - 1) < HEAD_DIM


def _strided_rows(ref, i, n, stride):
    if stride == 1:
        return ref[0, i * n:(i + 1) * n, :]
    return ref[0, pl.ds(i, n, stride=stride), :]


def _inproj_kernel(*refs, r, kinds, rotary, with_f, nsteps):
    it = iter(refs)
    x_ref, ng_ref, gain_ref, bd_ref = (next(it) for _ in range(4))
    w_refs = [next(it) for _ in kinds]
    if rotary:
        c_ref, s_ref, p_ref = next(it), next(it), next(it)
    if with_f:
        wf_ref = next(it)
    out_ref = next(it)
    if with_f:
        ft_ref = next(it)
    h_scr = next(it)
    if rotary:
        tab_scr = next(it)
    if r > 1:
        hs_scr = next(it)

    tl = ROW_TILE // r
    nsec = len(kinds)
    per_step = nsec // nsteps
    reps = MXU_DIM // LANES
    nchunk = INNER // MXU_DIM

    def prologue():
        x = x_ref[0]
        ms = jnp.mean(x * x, axis=-1, keepdims=True)
        h = x * lax.rsqrt(ms + EPS) * ng_ref[...]
        if r == 1:
            h_scr[...] = h.astype(BF16)
        else:
            for c in range(D_MODEL // LANES):
                hs_scr[c] = h[:, c * LANES:(c + 1) * LANES]
        for i in range(r):
            rows = slice(i * tl, (i + 1) * tl)
            if r > 1:
                for c in range(D_MODEL // LANES):
                    h_scr[rows, c * LANES:(c + 1) * LANES] = hs_scr[
                        c, pl.ds(i, tl, stride=r), :].astype(BF16)
            if rotary:
                tab_scr[0, rows, :] = _strided_rows(c_ref, i, tl, r)
                tab_scr[1, rows, :] = _strided_rows(s_ref, i, tl, r)
        if with_f:
            f = jnp.dot(h_scr[...], wf_ref[...], preferred_element_type=F32)
            ft_ref[0] = f.T

    def main(idx, cc):
        return jnp.dot(h_scr[...], w_refs[idx][:, cc * MXU_DIM:(cc + 1) * MXU_DIM],
                       preferred_element_type=F32)

    def sections(first):
        chunks = [(idx, cc) for idx in range(first, first + per_step) for cc in range(nchunk)]
        nxt = main(*chunks[0])
        for n, (idx, cc) in enumerate(chunks):
            cols = slice(cc * MXU_DIM, (cc + 1) * MXU_DIM)
            a = nxt
            if n + 1 < len(chunks):
                nxt = main(*chunks[n + 1])
            if kinds[idx] == "norm":
                ms = jnp.dot((a * a).astype(BF16), bd_ref[...], preferred_element_type=F32)
                rs = lax.rsqrt(ms + EPS)
                if rotary:
                    u = a * gain_ref[idx, :, cols]
                    up = jnp.dot(u.astype(BF16), p_ref[...], preferred_element_type=F32)
                    cosv = jnp.concatenate([tab_scr[0]] * reps, axis=1)
                    sinv = jnp.concatenate([tab_scr[1]] * reps, axis=1)
                    a = (u * cosv + up * sinv) * rs
                else:
                    a = a * rs * gain_ref[idx, :, cols]
            ab = a.astype(BF16)
            for i in range(r):
                for lt in range(reps):
                    out_ref[0, i, (idx - first) * N_TILES + cc * reps + lt] = ab[
                        i * tl:(i + 1) * tl, lt * LANES:(lt + 1) * LANES]

    if nsteps == 1:
        prologue()
        sections(0)
    else:
        j = pl.program_id(2)
        pl.when(j == 0)(prologue)
        for step in range(nsteps):
            pl.when(j == step)(functools.partial(sections, step * per_step))


def _head_mean_matrix():
    i = np.arange(MXU_DIM)
    return jnp.asarray((i[:, None] // HEAD_DIM == i[None, :] // HEAD_DIM) / HEAD_DIM, BF16)


def _rotary_partner_matrix():
    half = ROT_DIM // 2
    i = np.arange(MXU_DIM)
    d = i % HEAD_DIM
    partner = np.where(d < half, i + half, np.where(d < ROT_DIM, i - half, -1))
    return jnp.asarray(i[:, None] == partner[None, :], BF16)


def _inproj(x, norm_g, w, lj, sec_idx, gains, kinds, *, r, tabs=None, w_f=None):
    B, S, _ = x.shape
    L = S // r
    tl = ROW_TILE // r
    nsec = len(kinds)
    rotary = tabs is not None
    with_f = w_f is not None

    nsteps = nsec if rotary else 1
    row_map = lambda b, t, j: (b, t, 0)
    const2 = lambda b, t, j: (0, 0)
    const3 = lambda b, t, j: (0, 0, 0)
    in_specs = [
        pl.BlockSpec((1, ROW_TILE, D_MODEL), row_map),
        pl.BlockSpec((1, D_MODEL), const2),
        pl.BlockSpec((nsec, 1, INNER), const3),
        pl.BlockSpec((MXU_DIM, MXU_DIM), const2),
    ]
    args = [x, norm_g.reshape(1, D_MODEL), gains, _head_mean_matrix()]
    for s in sec_idx:
        in_specs.append(pl.BlockSpec((None, D_MODEL, INNER), lambda b, t, j, s=s: (lj, 0, s)))
        args.append(w)
    scratch = [pltpu.VMEM((ROW_TILE, D_MODEL), BF16)]
    if rotary:
        for t in tabs:
            in_specs.append(pl.BlockSpec((1, ROW_TILE, LANES), row_map))
            args.append(t)
        in_specs.append(pl.BlockSpec((MXU_DIM, MXU_DIM), const2))
        args.append(_rotary_partner_matrix())
        scratch.append(pltpu.VMEM((2, ROW_TILE, LANES), F32))
    if r > 1:
        scratch.append(pltpu.VMEM((D_MODEL // LANES, ROW_TILE, LANES), F32))
    if with_f:
        in_specs.append(pl.BlockSpec((None, D_MODEL, LANES), lambda b, t, j: (lj, 0, 0)))
        args.append(w_f)
    out_shape = [jax.ShapeDtypeStruct((B, r, nsec * N_TILES, L, LANES), BF16)]
    out_specs = [pl.BlockSpec((1, r, nsec // nsteps * N_TILES, tl, LANES),
                              lambda b, t, j: (b, 0, j, t, 0))]
    if with_f:
        out_shape.append(jax.ShapeDtypeStruct((B, LANES, S), F32))
        out_specs.append(pl.BlockSpec((1, LANES, ROW_TILE), lambda b, t, j: (b, 0, t)))
    return pl.pallas_call(
        functools.partial(_inproj_kernel, r=r, kinds=kinds, rotary=rotary, with_f=with_f,
                          nsteps=nsteps),
        grid=(B, S // ROW_TILE, nsteps),
        in_specs=in_specs,
        out_specs=out_specs,
        out_shape=out_shape,
        scratch_shapes=scratch,
        compiler_params=pltpu.CompilerParams(
            dimension_semantics=("arbitrary",) * 3, vmem_limit_bytes=VMEM_LIMIT),
        name=f"inproj_r{r}",
    )(*args)


def _fox_c_kernel(ft_ref, bf_ref, c_ref, *, blk):
    x = ft_ref[0] + bf_ref[...]
    ls = jnp.minimum(x, 0.0) - jnp.log1p(jnp.exp(-jnp.abs(x)))
    S = ls.shape[-1]
    tri = (lax.broadcasted_iota(jnp.int32, (blk, blk), 0)
           <= lax.broadcasted_iota(jnp.int32, (blk, blk), 1)).astype(BF16)
    carry = jnp.zeros((N_HEADS, 1), F32)
    for n in range(S // blk):
        t = ls[:, n * blk:(n + 1) * blk]
        h1 = t.astype(BF16)
        r1 = t - h1.astype(F32)
        h2 = r1.astype(BF16)
        h3 = (r1 - h2.astype(F32)).astype(BF16)
        cs = (jnp.dot(h1, tri, preferred_element_type=F32)
              + jnp.dot(h2, tri, preferred_element_type=F32)
              + jnp.dot(h3, tri, preferred_element_type=F32)) + carry
        c_ref[0, :, n * blk:(n + 1) * blk] = cs
        carry = cs[:, blk - 1:blk]


def _fox_c(ft, b_f):
    B, _, S = ft.shape
    return pl.pallas_call(
        functools.partial(_fox_c_kernel, blk=FOX_SUB),
        grid=(B,),
        in_specs=[pl.BlockSpec((1, N_HEADS, S), lambda b: (b, 0, 0)),
                  pl.BlockSpec((N_HEADS, 1), lambda b: (0, 0))],
        out_specs=pl.BlockSpec((1, N_HEADS, S), lambda b: (b, 0, 0)),
        out_shape=jax.ShapeDtypeStruct((B, N_HEADS, S), F32),
        compiler_params=pltpu.CompilerParams(dimension_semantics=("arbitrary",)),
        name="fox_c",
    )(ft, b_f.reshape(N_HEADS, 1))


def _silu(g):
    return g * jax.nn.sigmoid(g)


def _fox_attn_kernel(q_ref, k_ref, v_ref, g_ref, c_ref, o_ref, m_scr, acc_scr, va_scr,
                     *, tq, sub):
    hp = pl.program_id(1)
    S = k_ref.shape[3]
    nsub = S // sub
    per_tile = tq // sub
    low = _lane_is_low((tq, LANES))

    for n in range(S // tq):
        rows = slice(n * tq, (n + 1) * tq)
        vf = v_ref[0, 0, 0, rows, :].astype(F32)
        va_scr[0, rows, :] = jnp.where(low, vf, 1.0).astype(BF16)
        va_scr[1, rows, :] = jnp.where(low, 1.0, vf).astype(BF16)

    mask = (lax.broadcasted_iota(jnp.int32, (tq, sub), 1)
            <= lax.broadcasted_iota(jnp.int32, (tq, sub), 0))

    def run(nq):
        qrows = slice(nq * tq, (nq + 1) * tq)
        qf = q_ref[0, 0, 0, qrows, :].astype(F32)
        qm = (jnp.where(low, qf, 0.0).astype(BF16), jnp.where(low, 0.0, qf).astype(BF16))
        m_scr[nq] = jnp.full(m_scr.shape[1:], NEG, F32)
        acc_scr[nq] = jnp.zeros(acc_scr.shape[1:], F32)

        def bias_row(hh, blk0, nblk):
            crow = (hp * HEADS_PER_TILE + hh) * nsub
            c0 = c_ref[0, crow + nq * per_tile][:, 0:1]
            cs = jnp.concatenate([c_ref[0, crow + blk0 + n] for n in range(nblk)], axis=1)
            return (c0 - cs) * LOG2E

        def scores(r0, blk0, nblk):
            kt = k_ref[0, 0, 0, blk0 * sub:(blk0 + nblk) * sub, :]
            return tuple(
                lax.dot_general(qm[hh][r0:], kt, (((1,), (1,)), ((), ())),
                                preferred_element_type=F32) + bias_row(hh, blk0, nblk)
                for hh in range(HEADS_PER_TILE))

        def accumulate(s, hh, r0, nr, k0, w):
            rows = slice(r0, r0 + nr)
            m_prev = m_scr[nq, hh, rows, :]
            m_new = jnp.maximum(m_prev, jnp.max(s, axis=-1, keepdims=True))
            alpha = jnp.exp2(m_prev - m_new)
            p = jnp.exp2(s - jnp.concatenate([m_new] * (w // LANES), axis=1))
            acc_scr[nq, hh, rows, :] = alpha * acc_scr[nq, hh, rows, :] + jnp.dot(
                p.astype(BF16), va_scr[hh, k0:k0 + w, :], preferred_element_type=F32)
            m_scr[nq, hh, rows, :] = m_new

        steps = [(0, j * per_tile, per_tile, None) for j in range(nq)]
        steps += [(0, nq * per_tile, 1, mask), (sub, nq * per_tile + 1, 1, mask[:sub])]
        s_cur = scores(*steps[0][:3])
        for i, (r0, blk0, nblk, msk) in enumerate(steps):
            if i + 1 < len(steps):
                s_next = scores(*steps[i + 1][:3])
            for hh in range(HEADS_PER_TILE):
                s = s_cur[hh] if msk is None else jnp.where(msk, s_cur[hh], NEG)
                accumulate(s, hh, r0, tq - r0, blk0 * sub, nblk * sub)
            s_cur = s_next

        num = jnp.where(low, acc_scr[nq, 0], acc_scr[nq, 1])
        den = pltpu.roll(jnp.where(low, acc_scr[nq, 1], acc_scr[nq, 0]), HEAD_DIM, 1)
        gate = g_ref[0, 0, 0, qrows, :].astype(F32)
        o_ref[0, 0, qrows, :] = (num / den * _silu(gate)).astype(BF16)

    for nq in range(S // tq):
        run(nq)


def _fox_attn(qkvg, c):
    B, _, _, S, _ = qkvg.shape
    tq, sub = FOX_TQ, FOX_SUB
    assert tq == 2 * sub
    nrow = N_HEADS * (S // sub)
    c2 = c.reshape(B, nrow, 1, sub)
    return pl.pallas_call(
        functools.partial(_fox_attn_kernel, tq=tq, sub=sub),
        grid=(B, N_TILES),
        in_specs=[
            pl.BlockSpec((1, 1, 1, S, LANES), lambda b, hp, s=s: (b, 0, s * N_TILES + hp, 0, 0))
            for s in range(4)
        ] + [pl.BlockSpec((1, nrow, 1, sub), lambda b, hp: (b, 0, 0, 0))],
        out_specs=pl.BlockSpec((1, 1, S, LANES), lambda b, hp: (b, hp, 0, 0)),
        out_shape=jax.ShapeDtypeStruct((B, N_TILES, S, LANES), BF16),
        scratch_shapes=[pltpu.VMEM((S // tq, HEADS_PER_TILE, tq, LANES), F32),
                        pltpu.VMEM((S // tq, HEADS_PER_TILE, tq, LANES), F32),
                        pltpu.VMEM((HEADS_PER_TILE, S, LANES), BF16)],
        compiler_params=pltpu.CompilerParams(
            dimension_semantics=("arbitrary",) * 2, vmem_limit_bytes=VMEM_LIMIT),
        name="fox_attn",
    )(qkvg, qkvg, qkvg, qkvg, c2)


def _dsw_attn_kernel(*refs):
    (q0, k0, v0, gate_ref, q1, k1, v1, q2, k2, v2, o_ref, og_scr, lse_scr, *vx_scr) = refs
    groups = ((q0, k0, v0), (q1, k1, v1), (q2, k2, v2))
    S = o_ref.shape[2]
    low = _lane_is_low((BLOCK, LANES))
    row = lax.broadcasted_iota(jnp.int32, (BLOCK, 2 * BLOCK), 0)
    col = lax.broadcasted_iota(jnp.int32, (BLOCK, 2 * BLOCK), 1)
    band = (col >= row) & (col <= row + BLOCK)
    band0 = band & (col >= BLOCK)
    band2 = jnp.concatenate([band] * HEADS_PER_TILE, axis=0)
    band02 = jnp.concatenate([band0] * HEADS_PER_TILE, axis=0)

    def band_slices(n):
        cur = slice(n * BLOCK, (n + 1) * BLOCK)
        prev = slice(max(n - 1, 0) * BLOCK, (max(n - 1, 0) + 1) * BLOCK)
        return prev, cur

    one_lo = jnp.where(low, 1.0, 0.0).astype(BF16)
    one_hi = jnp.where(low, 0.0, 1.0).astype(BF16)
    for g, (_, r) in enumerate(DSWA_GROUPS):
        v_ref = groups[g][2]
        for z in range(r):
            for n in range(S // r // BLOCK):
                cur = slice(n * BLOCK, (n + 1) * BLOCK)
                vf = v_ref[0, z, 0, cur, :].astype(F32)
                vx_scr[g][0, z, cur, :LANES] = jnp.where(low, vf, 0.0).astype(BF16)
                vx_scr[g][0, z, cur, LANES:] = one_lo
                vx_scr[g][1, z, cur, :LANES] = jnp.where(low, 0.0, vf).astype(BF16)
                vx_scr[g][1, z, cur, LANES:] = one_hi

    def block(g, z, n):
        q_ref, k_ref, _ = groups[g]
        r = DSWA_GROUPS[g][1]
        prev, cur = band_slices(n)
        qf = q_ref[0, z, 0, cur, :].astype(F32)
        kb = jnp.concatenate([k_ref[0, z, 0, prev, :], k_ref[0, z, 0, cur, :]], axis=0)
        valid = band2 if n > 0 else band02
        qm = jnp.concatenate([jnp.where(low, qf, 0.0), jnp.where(low, 0.0, qf)],
                             axis=0).astype(BF16)
        s = lax.dot_general(qm, kb, (((1,), (1,)), ((), ())), preferred_element_type=F32)
        s = jnp.where(valid, s, NEG)
        m = jnp.max(s, axis=-1, keepdims=True)
        p = jnp.exp2(s - m).astype(BF16)
        vx = jnp.concatenate([vx_scr[g][hh, z, sl, :] for hh in range(HEADS_PER_TILE)
                              for sl in (prev, cur)], axis=0)
        ol = jnp.dot(jnp.concatenate([p[:BLOCK], p[BLOCK:]], axis=1), vx,
                     preferred_element_type=F32)
        l = ol[:, LANES:]
        dst = pl.ds(z + r * n * BLOCK, BLOCK, stride=r) if r > 1 else cur
        og_scr[g, dst, :] = ol[:, :LANES] / l
        lse_scr[g, dst, :] = jnp.where(low, m[:BLOCK], m[BLOCK:]) + jnp.log2(l)

    for g, (_, r) in enumerate(DSWA_GROUPS):
        for z in range(r):
            for n in range(S // r // BLOCK):
                block(g, z, n)

    chunk = 2 * BLOCK

    def combine(i, carry):
        rows = pl.ds(pl.multiple_of(i * chunk, chunk), chunk)
        l0, l1, l2 = lse_scr[0, rows, :], lse_scr[1, rows, :], lse_scr[2, rows, :]
        mx = jnp.maximum(jnp.maximum(l0, l1), l2)
        e0, e1, e2 = jnp.exp2(l0 - mx), jnp.exp2(l1 - mx), jnp.exp2(l2 - mx)
        o = (e0 * og_scr[0, rows, :] + e1 * og_scr[1, rows, :] + e2 * og_scr[2, rows, :]) / (
            e0 + e1 + e2)
        o_ref[0, 0, rows, :] = (o * _silu(gate_ref[0, 0, 0, rows, :].astype(F32))).astype(BF16)
        return carry

    lax.fori_loop(0, S // chunk, combine, 0)


def _dsw_attn(p0, p1, p2):
    B, _, _, S, _ = p0.shape
    in_specs, args = [], []
    for g, (p, (_, r)) in enumerate(zip((p0, p1, p2), DSWA_GROUPS)):
        for sec in range(3):
            in_specs.append(pl.BlockSpec(
                (1, r, 1, S // r, LANES),
                lambda b, hp, sec=sec: (b, 0, sec * N_TILES + hp, 0, 0)))
            args.append(p)
        if g == 0:
            in_specs.append(pl.BlockSpec(
                (1, 1, 1, S, LANES), lambda b, hp: (b, 0, 3 * N_TILES + hp, 0, 0)))
            args.append(p)
    return pl.pallas_call(
        _dsw_attn_kernel,
        grid=(B, N_TILES),
        in_specs=in_specs,
        out_specs=pl.BlockSpec((1, 1, S, LANES), lambda b, hp: (b, hp, 0, 0)),
        out_shape=jax.ShapeDtypeStruct((B, N_TILES, S, LANES), BF16),
        scratch_shapes=[pltpu.VMEM((3, S, LANES), F32), pltpu.VMEM((3, S, LANES), F32)] + [
            pltpu.VMEM((HEADS_PER_TILE, r, S // r, 2 * LANES), BF16) for _, r in DSWA_GROUPS],
        compiler_params=pltpu.CompilerParams(
            dimension_semantics=("arbitrary",) * 2, vmem_limit_bytes=VMEM_LIMIT),
        name="dsw_attn",
    )(*args)


def _outproj_kernel(a_ref, w_ref, x_ref, o_ref):
    a = jnp.concatenate([a_ref[0, c] for c in range(N_TILES)], axis=1)
    o_ref[0] = x_ref[0] + jnp.dot(a, w_ref[...], preferred_element_type=F32)


def _outproj(a, w, lj, x):
    B, S, _ = x.shape
    return pl.pallas_call(
        _outproj_kernel,
        grid=(B, S // OUT_ROW_TILE),
        in_specs=[pl.BlockSpec((1, N_TILES, OUT_ROW_TILE, LANES), lambda b, t: (b, 0, t, 0)),
                  pl.BlockSpec((None, INNER, D_MODEL), lambda b, t: (lj, 0, 0)),
                  pl.BlockSpec((1, OUT_ROW_TILE, D_MODEL), lambda b, t: (b, t, 0))],
        out_specs=pl.BlockSpec((1, OUT_ROW_TILE, D_MODEL), lambda b, t: (b, t, 0)),
        out_shape=jax.ShapeDtypeStruct((B, S, D_MODEL), F32),
        compiler_params=pltpu.CompilerParams(
            dimension_semantics=("arbitrary",) * 2, vmem_limit_bytes=VMEM_LIMIT),
        name="outproj",
    )(a, w, x)


def _gain_rows(q_g, k_g, n_plain):
    rows = [jnp.tile(q_g * (SCALE * LOG2E), N_HEADS), jnp.tile(k_g, N_HEADS)]
    rows += [jnp.ones((INNER,), F32)] * n_plain
    return jnp.stack(rows, axis=0).reshape(len(rows), 1, INNER).astype(F32)


def _rotary_tables(positions):
    half = ROT_DIM // 2
    inv_freq = ROPE_THETA ** (-jnp.arange(0, ROT_DIM, 2, dtype=F32) / ROT_DIM)
    ang = positions.astype(F32)[..., None] * inv_freq
    cos, sin = jnp.cos(ang), jnp.sin(ang)
    B, S = positions.shape
    one = jnp.ones((B, S, HEAD_DIM - ROT_DIM), F32)
    zero = jnp.zeros((B, S, HEAD_DIM - ROT_DIM), F32)
    c = jnp.concatenate([cos, cos, one], axis=-1)
    s = jnp.concatenate([-sin, sin, zero], axis=-1)
    return tuple(jnp.tile(t, (1, 1, HEADS_PER_TILE)) for t in (c, s))


def kernel(x, positions, norm_g, fox_w_in, fox_b_f, fox_q_norm, fox_k_norm, fox_w_out,
           dsw_w_in, dsw_q_norm, dsw_k_norm, dsw_w_out):
    B, S, _ = x.shape
    depth = norm_g.shape[0]
    tabs = _rotary_tables(positions)
    fox_w = fox_w_in.astype(BF16)
    fox_wf = jnp.pad(fox_w[:, :, 4 * INNER:], ((0, 0), (0, 0), (0, LANES - N_HEADS)))
    dsw_w = dsw_w_in.astype(BF16)
    fox_wo = fox_w_out.astype(BF16)
    dsw_wo = dsw_w_out.astype(BF16)
    for i in range(depth):
        j = i // 2
        if i % 2 == 0:
            gains = _gain_rows(fox_q_norm[j], fox_k_norm[j], 2)
            qkvg, ft = _inproj(x, norm_g[i], fox_w, j, (0, 1, 2, 3), gains,
                               ("norm", "norm", "plain", "plain"), r=1, w_f=fox_wf)
            c = _fox_c(ft, fox_b_f[j])
            a = _fox_attn(qkvg, c)
            x = _outproj(a, fox_wo, j, x)
        else:
            projs = []
            n_groups = len(DSWA_GROUPS)
            for g, (_, r) in enumerate(DSWA_GROUPS):
                idxs = (g, n_groups + g, 2 * n_groups + g) + ((3 * n_groups,) if g == 0 else ())
                kinds = ("norm", "norm", "plain") + (("plain",) if g == 0 else ())
                gains = _gain_rows(dsw_q_norm[j, g], dsw_k_norm[j, g], len(kinds) - 2)
                projs.append(_inproj(x, norm_g[i], dsw_w, j, idxs, gains, kinds,
                                     r=r, tabs=tabs)[0])
            a = _dsw_attn(*projs)
            x = _outproj(a, dsw_wo, j, x)
    return x
```

```python
import functools

import jax
import jax.numpy as jnp
import numpy as np
from jax import lax
from jax.experimental import pallas as pl
from jax.experimental.pallas import tpu as pltpu

D_MODEL = 1024
HEAD_DIM = 64
N_HEADS = 16
INNER = N_HEADS * HEAD_DIM
ROT_DIM = HEAD_DIM // 4
ROPE_THETA = 500000.0
DSWA_GROUPS = ((128, 1), (512, 4), (2048, 16))
BLOCK = 128
EPS = 1e-6
SCALE = HEAD_DIM ** -0.5
LOG2E = 1.4426950408889634

LANES = 128
MXU_DIM = 256
HEADS_PER_TILE = LANES // HEAD_DIM
N_TILES = INNER // LANES
NEG = -1e30
VMEM_LIMIT = 58 * 1024 * 1024

ROW_TILE = 1024
OUT_ROW_TILE = 512
FOX_TQ = 512
FOX_SUB = 256

F32 = jnp.float32
BF16 = jnp.bfloat16


def _lane_is_low(shape):
    return lax.broadcasted_iota(jnp.int32, shape, len(shape) - 1) < HEAD_DIM


def _strided_rows(ref, i, n, stride):
    if stride == 1:
        return ref[0, i * n:(i + 1) * n, :]
    return ref[0, pl.ds(i, n, stride=stride), :]


def _inproj_kernel(*refs, r, kinds, rotary, with_f, nsteps):
    it = iter(refs)
    x_ref, ng_ref, gain_ref, bd_ref = (next(it) for _ in range(4))
    w_refs = [next(it) for _ in kinds]
    if rotary:
        c_ref, s_ref, p_ref = next(it), next(it), next(it)
    if with_f:
        wf_ref = next(it)
    out_ref = next(it)
    if with_f:
        ft_ref = next(it)
    h_scr = next(it)
    if rotary:
        tab_scr = next(it)
    if r > 1:
        hs_scr = next(it)

    tl = ROW_TILE // r
    nsec = len(kinds)
    per_step = nsec // nsteps
    reps = MXU_DIM // LANES
    nchunk = INNER // MXU_DIM

    def prologue():
        x = x_ref[0]
        ms = jnp.mean(x * x, axis=-1, keepdims=True)
        h = x * lax.rsqrt(ms + EPS) * ng_ref[...]
        if r == 1:
            h_scr[...] = h.astype(BF16)
        else:
            for c in range(D_MODEL // LANES):
                hs_scr[c] = h[:, c * LANES:(c + 1) * LANES]
        for i in range(r):
            rows = slice(i * tl, (i + 1) * tl)
            if r > 1:
                for c in range(D_MODEL // LANES):
                    h_scr[rows, c * LANES:(c + 1) * LANES] = hs_scr[
                        c, pl.ds(i, tl, stride=r), :].astype(BF16)
            if rotary:
                tab_scr[0, rows, :] = _strided_rows(c_ref, i, tl, r)
                tab_scr[1, rows, :] = _strided_rows(s_ref, i, tl, r)
        if with_f:
            f = jnp.dot(h_scr[...], wf_ref[...], preferred_element_type=F32)
            ft_ref[0] = f.T

    def main(idx, cc):
        return jnp.dot(h_scr[...], w_refs[idx][:, cc * MXU_DIM:(cc + 1) * MXU_DIM],
                       preferred_element_type=F32)

    def sections(first):
        chunks = [(idx, cc) for idx in range(first, first + per_step) for cc in range(nchunk)]
        nxt = main(*chunks[0])
        for n, (idx, cc) in enumerate(chunks):
            cols = slice(cc * MXU_DIM, (cc + 1) * MXU_DIM)
            a = nxt
            if n + 1 < len(chunks):
                nxt = main(*chunks[n + 1])
            if kinds[idx] == "norm":
                ms = jnp.dot((a * a).astype(BF16), bd_ref[...], preferred_element_type=F32)
                rs = lax.rsqrt(ms + EPS)
                if rotary:
                    u = a * gain_ref[idx, :, cols]
                    up = jnp.dot(u.astype(BF16), p_ref[...], preferred_element_type=F32)
                    cosv = jnp.concatenate([tab_scr[0]] * reps, axis=1)
                    sinv = jnp.concatenate([tab_scr[1]] * reps, axis=1)
                    a = (u * cosv + up * sinv) * rs
                else:
                    a = a * rs * gain_ref[idx, :, cols]
            ab = a.astype(BF16)
            for i in range(r):
                for lt in range(reps):
                    out_ref[0, i, (idx - first) * N_TILES + cc * reps + lt] = ab[
                        i * tl:(i + 1) * tl, lt * LANES:(lt + 1) * LANES]

    if nsteps == 1:
        prologue()
        sections(0)
    else:
        j = pl.program_id(2)
        pl.when(j == 0)(prologue)
        for step in range(nsteps):
            pl.when(j == step)(functools.partial(sections, step * per_step))


def _head_mean_matrix():
    i = np.arange(MXU_DIM)
    return jnp.asarray((i[:, None] // HEAD_DIM == i[None, :] // HEAD_DIM) / HEAD_DIM, BF16)


def _rotary_partner_matrix():
    half = ROT_DIM // 2
    i = np.arange(MXU_DIM)
    d = i % HEAD_DIM
    partner = np.where(d < half, i + half, np.where(d < ROT_DIM, i - half, -1))
    return jnp.asarray(i[:, None] == partner[None, :], BF16)


def _inproj(x, norm_g, w, lj, sec_idx, gains, kinds, *, r, tabs=None, w_f=None):
    B, S, _ = x.shape
    L = S // r
    tl = ROW_TILE // r
    nsec = len(kinds)
    rotary = tabs is not None
    with_f = w_f is not None

    nsteps = nsec if r > 1 else 1
    row_map = lambda b, t, j: (b, t, 0)
    const2 = lambda b, t, j: (0, 0)
    const3 = lambda b, t, j: (0, 0, 0)
    in_specs = [
        pl.BlockSpec((1, ROW_TILE, D_MODEL), row_map),
        pl.BlockSpec((1, D_MODEL), const2),
        pl.BlockSpec((nsec, 1, INNER), const3),
        pl.BlockSpec((MXU_DIM, MXU_DIM), const2),
    ]
    args = [x, norm_g.reshape(1, D_MODEL), gains, _head_mean_matrix()]
    for s in sec_idx:
        in_specs.append(pl.BlockSpec((None, D_MODEL, INNER), lambda b, t, j, s=s: (lj, 0, s)))
        args.append(w)
    scratch = [pltpu.VMEM((ROW_TILE, D_MODEL), BF16)]
    if rotary:
        for t in tabs:
            in_specs.append(pl.BlockSpec((1, ROW_TILE, LANES), row_map))
            args.append(t)
        in_specs.append(pl.BlockSpec((MXU_DIM, MXU_DIM), const2))
        args.append(_rotary_partner_matrix())
        scratch.append(pltpu.VMEM((2, ROW_TILE, LANES), F32))
    if r > 1:
        scratch.append(pltpu.VMEM((D_MODEL // LANES, ROW_TILE, LANES), F32))
    if with_f:
        in_specs.append(pl.BlockSpec((None, D_MODEL, LANES), lambda b, t, j: (lj, 0, 0)))
        args.append(w_f)
    out_shape = [jax.ShapeDtypeStruct((B, r, nsec * N_TILES, L, LANES), BF16)]
    out_specs = [pl.BlockSpec((1, r, nsec // nsteps * N_TILES, tl, LANES),
                              lambda b, t, j: (b, 0, j, t, 0))]
    if with_f:
        out_shape.append(jax.ShapeDtypeStruct((B, LANES, S), F32))
        out_specs.append(pl.BlockSpec((1, LANES, ROW_TILE), lambda b, t, j: (b, 0, t)))
    return pl.pallas_call(
        functools.partial(_inproj_kernel, r=r, kinds=kinds, rotary=rotary, with_f=with_f,
                          nsteps=nsteps),
        grid=(B, S // ROW_TILE, nsteps),
        in_specs=in_specs,
        out_specs=out_specs,
        out_shape=out_shape,
        scratch_shapes=scratch,
        compiler_params=pltpu.CompilerParams(
            dimension_semantics=("arbitrary",) * 3, vmem_limit_bytes=VMEM_LIMIT),
        name=f"inproj_r{r}",
    )(*args)


def _fox_c_kernel(ft_ref, bf_ref, c_ref, *, blk):
    x = ft_ref[0] + bf_ref[...]
    ls = jnp.minimum(x, 0.0) - jnp.log1p(jnp.exp(-jnp.abs(x)))
    S = ls.shape[-1]
    tri = (lax.broadcasted_iota(jnp.int32, (blk, blk), 0)
           <= lax.broadcasted_iota(jnp.int32, (blk, blk), 1)).astype(BF16)
    carry = jnp.zeros((N_HEADS, 1), F32)
    for n in range(S // blk):
        t = ls[:, n * blk:(n + 1) * blk]
        h1 = t.astype(BF16)
        r1 = t - h1.astype(F32)
        h2 = r1.astype(BF16)
        h3 = (r1 - h2.astype(F32)).astype(BF16)
        cs = (jnp.dot(h1, tri, preferred_element_type=F32)
              + jnp.dot(h2, tri, preferred_element_type=F32)
              + jnp.dot(h3, tri, preferred_element_type=F32)) + carry
        c_ref[0, :, n * blk:(n + 1) * blk] = cs
        carry = cs[:, blk - 1:blk]


def _fox_c(ft, b_f):
    B, _, S = ft.shape
    return pl.pallas_call(
        functools.partial(_fox_c_kernel, blk=FOX_SUB),
        grid=(B,),
        in_specs=[pl.BlockSpec((1, N_HEADS, S), lambda b: (b, 0, 0)),
                  pl.BlockSpec((N_HEADS, 1), lambda b: (0, 0))],
        out_specs=pl.BlockSpec((1, N_HEADS, S), lambda b: (b, 0, 0)),
        out_shape=jax.ShapeDtypeStruct((B, N_HEADS, S), F32),
        compiler_params=pltpu.CompilerParams(dimension_semantics=("arbitrary",)),
        name="fox_c",
    )(ft, b_f.reshape(N_HEADS, 1))


def _fox_attn_kernel(q_ref, k_ref, v_ref, g_ref, c_ref, o_ref, m_scr, acc_scr, va_scr,
                     *, tq, sub):
    hp = pl.program_id(1)
    S = k_ref.shape[3]
    nsub = S // sub
    per_tile = tq // sub
    low = _lane_is_low((tq, LANES))

    for n in range(S // tq):
        rows = slice(n * tq, (n + 1) * tq)
        vf = v_ref[0, 0, 0, rows, :].astype(F32)
        va_scr[0, rows, :] = jnp.where(low, vf, 1.0).astype(BF16)
        va_scr[1, rows, :] = jnp.where(low, 1.0, vf).astype(BF16)

    mask = (lax.broadcasted_iota(jnp.int32, (tq, sub), 1)
            <= lax.broadcasted_iota(jnp.int32, (tq, sub), 0))

    def run(nq):
        qrows = slice(nq * tq, (nq + 1) * tq)
        qf = q_ref[0, 0, 0, qrows, :].astype(F32)
        qm = (jnp.where(low, qf, 0.0).astype(BF16), jnp.where(low, 0.0, qf).astype(BF16))
        m_scr[nq] = jnp.full(m_scr.shape[1:], NEG, F32)
        acc_scr[nq] = jnp.zeros(acc_scr.shape[1:], F32)

        def bias_row(hh, blk0, nblk):
            crow = (hp * HEADS_PER_TILE + hh) * nsub
            c0 = c_ref[0, crow + nq * per_tile][:, 0:1]
            cs = jnp.concatenate([c_ref[0, crow + blk0 + n] for n in range(nblk)], axis=1)
            return (c0 - cs) * LOG2E

        def scores(r0, blk0, nblk):
            kt = k_ref[0, 0, 0, blk0 * sub:(blk0 + nblk) * sub, :]
            return tuple(
                lax.dot_general(qm[hh][r0:], kt, (((1,), (1,)), ((), ())),
                                preferred_element_type=F32) + bias_row(hh, blk0, nblk)
                for hh in range(HEADS_PER_TILE))

        def accumulate(s, hh, r0, nr, k0, w):
            rows = slice(r0, r0 + nr)
            m_prev = m_scr[nq, hh, rows, :]
            m_new = jnp.maximum(m_prev, jnp.max(s, axis=-1, keepdims=True))
            alpha = jnp.exp2(m_prev - m_new)
            p = jnp.exp2(s - jnp.concatenate([m_new] * (w // LANES), axis=1))
            acc_scr[nq, hh, rows, :] = alpha * acc_scr[nq, hh, rows, :] + jnp.dot(
                p.astype(BF16), va_scr[hh, k0:k0 + w, :], preferred_element_type=F32)
            m_scr[nq, hh, rows, :] = m_new

        steps = [(0, j * per_tile, per_tile, None) for j in range(nq)]
        steps += [(0, nq * per_tile, 1, mask), (sub, nq * per_tile + 1, 1, mask[:sub])]
        s_cur = scores(*steps[0][:3])
        for i, (r0, blk0, nblk, msk) in enumerate(steps):
            if i + 1 < len(steps):
                s_next = scores(*steps[i + 1][:3])
            for hh in range(HEADS_PER_TILE):
                s = s_cur[hh] if msk is None else jnp.where(msk, s_cur[hh], NEG)
                accumulate(s, hh, r0, tq - r0, blk0 * sub, nblk * sub)
            s_cur = s_next

        num = jnp.where(low, acc_scr[nq, 0], acc_scr[nq, 1])
        den = pltpu.roll(jnp.where(low, acc_scr[nq, 1], acc_scr[nq, 0]), HEAD_DIM, 1)
        gate = g_ref[0, 0, 0, qrows, :].astype(F32)
        o_ref[0, 0, qrows, :] = (num * gate / (den * (1.0 + jnp.exp(-gate)))).astype(BF16)

    for nq in range(S // tq):
        run(nq)


def _fox_attn(qkvg, c):
    B, _, _, S, _ = qkvg.shape
    tq, sub = FOX_TQ, FOX_SUB
    assert tq == 2 * sub
    nrow = N_HEADS * (S // sub)
    c2 = c.reshape(B, nrow, 1, sub)
    return pl.pallas_call(
        functools.partial(_fox_attn_kernel, tq=tq, sub=sub),
        grid=(B, N_TILES),
        in_specs=[
            pl.BlockSpec((1, 1, 1, S, LANES), lambda b, hp, s=s: (b, 0, s * N_TILES + hp, 0, 0))
            for s in range(4)
        ] + [pl.BlockSpec((1, nrow, 1, sub), lambda b, hp: (b, 0, 0, 0))],
        out_specs=pl.BlockSpec((1, 1, S, LANES), lambda b, hp: (b, hp, 0, 0)),
        out_shape=jax.ShapeDtypeStruct((B, N_TILES, S, LANES), BF16),
        scratch_shapes=[pltpu.VMEM((S // tq, HEADS_PER_TILE, tq, LANES), F32),
                        pltpu.VMEM((S // tq, HEADS_PER_TILE, tq, LANES), F32),
                        pltpu.VMEM((HEADS_PER_TILE, S, LANES), BF16)],
        compiler_params=pltpu.CompilerParams(
            dimension_semantics=("arbitrary",) * 2, vmem_limit_bytes=VMEM_LIMIT),
        name="fox_attn",
    )(qkvg, qkvg, qkvg, qkvg, c2)


def _dsw_attn_kernel(*refs):
    (q0, k0, v0, gate_ref, q1, k1, v1, q2, k2, v2, o_ref, og_scr, lse_scr, *vx_scr) = refs
    groups = ((q0, k0, v0), (q1, k1, v1), (q2, k2, v2))
    S = o_ref.shape[2]
    low = _lane_is_low((BLOCK, LANES))
    row = lax.broadcasted_iota(jnp.int32, (BLOCK, 2 * BLOCK), 0)
    col = lax.broadcasted_iota(jnp.int32, (BLOCK, 2 * BLOCK), 1)
    band = (col >= row) & (col <= row + BLOCK)
    band0 = band & (col >= BLOCK)
    band2 = jnp.concatenate([band] * HEADS_PER_TILE, axis=0)
    band02 = jnp.concatenate([band0] * HEADS_PER_TILE, axis=0)

    def band_slices(n):
        cur = slice(n * BLOCK, (n + 1) * BLOCK)
        prev = slice(max(n - 1, 0) * BLOCK, (max(n - 1, 0) + 1) * BLOCK)
        return prev, cur

    one_lo = jnp.where(low, 1.0, 0.0).astype(BF16)
    one_hi = jnp.where(low, 0.0, 1.0).astype(BF16)
    for g, (_, r) in enumerate(DSWA_GROUPS):
        v_ref = groups[g][2]
        for z in range(r):
            for n in range(S // r // BLOCK):
                cur = slice(n * BLOCK, (n + 1) * BLOCK)
                vf = v_ref[0, z, 0, cur, :].astype(F32)
                vx_scr[g][0, z, cur, :LANES] = jnp.where(low, vf, 0.0).astype(BF16)
                vx_scr[g][0, z, cur, LANES:] = one_lo
                vx_scr[g][1, z, cur, :LANES] = jnp.where(low, 0.0, vf).astype(BF16)
                vx_scr[g][1, z, cur, LANES:] = one_hi

    def scores(g, z, n):
        q_ref, k_ref, _ = groups[g]
        prev, cur = band_slices(n)
        qf = q_ref[0, z, 0, cur, :].astype(F32)
        kb = jnp.concatenate([k_ref[0, z, 0, prev, :], k_ref[0, z, 0, cur, :]], axis=0)
        valid = band2 if n > 0 else band02
        qm = jnp.concatenate([jnp.where(low, qf, 0.0), jnp.where(low, 0.0, qf)],
                             axis=0).astype(BF16)
        s = lax.dot_general(qm, kb, (((1,), (1,)), ((), ())), preferred_element_type=F32)
        return jnp.where(valid, s, NEG)

    def finish(s, g, z, n):
        r = DSWA_GROUPS[g][1]
        prev, cur = band_slices(n)
        m = jnp.max(s, axis=-1, keepdims=True)
        p = jnp.exp2(s - m).astype(BF16)
        vx = jnp.concatenate([vx_scr[g][hh, z, sl, :] for hh in range(HEADS_PER_TILE)
                              for sl in (prev, cur)], axis=0)
        ol = jnp.dot(jnp.concatenate([p[:BLOCK], p[BLOCK:]], axis=1), vx,
                     preferred_element_type=F32)
        l = ol[:, LANES:]
        og = ol[:, :LANES] / l
        lse = jnp.where(low, m[:BLOCK], m[BLOCK:]) + jnp.log2(l)
        if r > 1:
            dst = pl.ds(z + r * n * BLOCK, BLOCK, stride=r)
            og_scr[g - 1, dst, :] = og
            lse_scr[g - 1, dst, :] = lse
            return
        l1, l2 = lse_scr[0, cur, :], lse_scr[1, cur, :]
        mx = jnp.maximum(jnp.maximum(lse, l1), l2)
        e0, e1, e2 = jnp.exp2(lse - mx), jnp.exp2(l1 - mx), jnp.exp2(l2 - mx)
        num = e0 * og + e1 * og_scr[0, cur, :] + e2 * og_scr[1, cur, :]
        gate = gate_ref[0, 0, 0, cur, :].astype(F32)
        den = (e0 + e1 + e2) * (1.0 + jnp.exp(-gate))
        o_ref[0, 0, cur, :] = (num * gate / den).astype(BF16)

    blocks = [(g, z, n) for g, (_, r) in reversed(list(enumerate(DSWA_GROUPS)))
              for z in range(r) for n in range(S // r // BLOCK)]
    nxt = scores(*blocks[0])
    for i, blk in enumerate(blocks):
        s = nxt
        if i + 1 < len(blocks):
            nxt = scores(*blocks[i + 1])
        finish(s, *blk)


def _dsw_attn(p0, p1, p2):
    B, _, _, S, _ = p0.shape
    assert DSWA_GROUPS[0][1] == 1 and all(r > 1 for _, r in DSWA_GROUPS[1:])
    n_dilated = len(DSWA_GROUPS) - 1
    in_specs, args = [], []
    for g, (p, (_, r)) in enumerate(zip((p0, p1, p2), DSWA_GROUPS)):
        for sec in range(3):
            in_specs.append(pl.BlockSpec(
                (1, r, 1, S // r, LANES),
                lambda b, hp, sec=sec: (b, 0, sec * N_TILES + hp, 0, 0)))
            args.append(p)
        if g == 0:
            in_specs.append(pl.BlockSpec(
                (1, 1, 1, S, LANES), lambda b, hp: (b, 0, 3 * N_TILES + hp, 0, 0)))
            args.append(p)
    return pl.pallas_call(
        _dsw_attn_kernel,
        grid=(B, N_TILES),
        in_specs=in_specs,
        out_specs=pl.BlockSpec((1, 1, S, LANES), lambda b, hp: (b, hp, 0, 0)),
        out_shape=jax.ShapeDtypeStruct((B, N_TILES, S, LANES), BF16),
        scratch_shapes=[pltpu.VMEM((n_dilated, S, LANES), F32),
                        pltpu.VMEM((n_dilated, S, LANES), F32)] + [
            pltpu.VMEM((HEADS_PER_TILE, r, S // r, 2 * LANES), BF16) for _, r in DSWA_GROUPS],
        compiler_params=pltpu.CompilerParams(
            dimension_semantics=("arbitrary",) * 2, vmem_limit_bytes=VMEM_LIMIT),
        name="dsw_attn",
    )(*args)


def _outproj_kernel(a_ref, w_ref, x_ref, o_ref):
    a = jnp.concatenate([a_ref[0, c] for c in range(N_TILES)], axis=1)
    o_ref[0] = x_ref[0] + jnp.dot(a, w_ref[...], preferred_element_type=F32)


def _outproj(a, w, lj, x):
    B, S, _ = x.shape
    return pl.pallas_call(
        _outproj_kernel,
        grid=(B, S // OUT_ROW_TILE),
        in_specs=[pl.BlockSpec((1, N_TILES, OUT_ROW_TILE, LANES), lambda b, t: (b, 0, t, 0)),
                  pl.BlockSpec((None, INNER, D_MODEL), lambda b, t: (lj, 0, 0)),
                  pl.BlockSpec((1, OUT_ROW_TILE, D_MODEL), lambda b, t: (b, t, 0))],
        out_specs=pl.BlockSpec((1, OUT_ROW_TILE, D_MODEL), lambda b, t: (b, t, 0)),
        out_shape=jax.ShapeDtypeStruct((B, S, D_MODEL), F32),
        compiler_params=pltpu.CompilerParams(
            dimension_semantics=("arbitrary",) * 2, vmem_limit_bytes=VMEM_LIMIT),
        name="outproj",
    )(a, w, x)


def _gain_rows(q_g, k_g, n_plain):
    rows = [jnp.tile(q_g * (SCALE * LOG2E), N_HEADS), jnp.tile(k_g, N_HEADS)]
    rows += [jnp.ones((INNER,), F32)] * n_plain
    return jnp.stack(rows, axis=0).reshape(len(rows), 1, INNER).astype(F32)


def _rotary_tables(positions):
    half = ROT_DIM // 2
    inv_freq = ROPE_THETA ** (-jnp.arange(0, ROT_DIM, 2, dtype=F32) / ROT_DIM)
    ang = positions.astype(F32)[..., None] * inv_freq
    cos, sin = jnp.cos(ang), jnp.sin(ang)
    B, S = positions.shape
    one = jnp.ones((B, S, HEAD_DIM - ROT_DIM), F32)
    zero = jnp.zeros((B, S, HEAD_DIM - ROT_DIM), F32)
    c = jnp.concatenate([cos, cos, one], axis=-1)
    s = jnp.concatenate([-sin, sin, zero], axis=-1)
    return tuple(jnp.tile(t, (1, 1, HEADS_PER_TILE)) for t in (c, s))


def kernel(x, positions, norm_g, fox_w_in, fox_b_f, fox_q_norm, fox_k_norm, fox_w_out,
           dsw_w_in, dsw_q_norm, dsw_k_norm, dsw_w_out):
    B, S, _ = x.shape
    depth = norm_g.shape[0]
    tabs = _rotary_tables(positions)
    fox_w = fox_w_in.astype(BF16)
    fox_wf = jnp.pad(fox_w[:, :, 4 * INNER:], ((0, 0), (0, 0), (0, LANES - N_HEADS)))
    dsw_w = dsw_w_in.astype(BF16)
    fox_wo = fox_w_out.astype(BF16)
    dsw_wo = dsw_w_out.astype(BF16)
    for i in range(depth):
        j = i // 2
        if i % 2 == 0:
            gains = _gain_rows(fox_q_norm[j], fox_k_norm[j], 2)
            qkvg, ft = _inproj(x, norm_g[i], fox_w, j, (0, 1, 2, 3), gains,
                               ("norm", "norm", "plain", "plain"), r=1, w_f=fox_wf)
            c = _fox_c(ft, fox_b_f[j])
            a = _fox_attn(qkvg, c)
            x = _outproj(a, fox_wo, j, x)
        else:
            projs = []
            n_groups = len(DSWA_GROUPS)
            for g, (_, r) in enumerate(DSWA_GROUPS):
                idxs = (g, n_groups + g, 2 * n_groups + g) + ((3 * n_groups,) if g == 0 else ())
                kinds = ("norm", "norm", "plain") + (("plain",) if g == 0 else ())
                gains = _gain_rows(dsw_q_norm[j, g], dsw_k_norm[j, g], len(kinds) - 2)
                projs.append(_inproj(x, norm_g[i], dsw_w, j, idxs, gains, kinds,
                                     r=r, tabs=tabs)[0])
            a = _dsw_attn(*projs)
            x = _outproj(a, dsw_wo, j, x)
    return x
```

```python
import functools

import jax
import jax.numpy as jnp
import numpy as np
from jax import lax
from jax.experimental import pallas as pl
from jax.experimental.pallas import tpu as pltpu

D_MODEL = 1024
HEAD_DIM = 64
N_HEADS = 16
INNER = N_HEADS * HEAD_DIM
ROT_DIM = HEAD_DIM // 4
ROPE_THETA = 500000.0
DSWA_GROUPS = ((128, 1), (512, 4), (2048, 16))
BLOCK = 128
EPS = 1e-6
SCALE = HEAD_DIM ** -0.5
LOG2E = 1.4426950408889634

LANES = 128
MXU_DIM = 256
HEADS_PER_TILE = LANES // HEAD_DIM
N_TILES = INNER // LANES
NEG = -1e30
VMEM_LIMIT = 58 * 1024 * 1024

ROW_TILE = 1024
OUT_ROW_TILE = 1024
FOX_TQ = 512
FOX_SUB = 256

F32 = jnp.float32
BF16 = jnp.bfloat16


def _lane_is_low(shape):
    return lax.broadcasted_iota(jnp.int32, shape, len(shape) - 1) < HEAD_DIM


def _strided_rows(ref, i, n, stride):
    if stride == 1:
        return ref[0, i * n:(i + 1) * n, :]
    return ref[0, pl.ds(i, n, stride=stride), :]


def _inproj_kernel(*refs, r, kinds, rotary, with_f):
    it = iter(refs)
    x_ref, ng_ref, gain_ref = (next(it) for _ in range(3))
    w_refs = [next(it) for _ in kinds]
    if rotary:
        c_ref, s_ref, p_ref = next(it), next(it), next(it)
    if with_f:
        wf_ref = next(it)
    out_ref = next(it)
    if with_f:
        ft_ref = next(it)
    h_scr = next(it)
    if rotary:
        tab_scr = next(it)
    if r > 1:
        hs_scr = next(it)

    tl = ROW_TILE // r
    reps = MXU_DIM // LANES
    nchunk = INNER // MXU_DIM

    x = x_ref[0]
    ms = jnp.mean(x * x, axis=-1, keepdims=True)
    h = x * lax.rsqrt(ms + EPS) * ng_ref[...]
    if r == 1:
        h_scr[...] = h.astype(BF16)
    else:
        for c in range(D_MODEL // LANES):
            hs_scr[c] = h[:, c * LANES:(c + 1) * LANES]
    for i in range(r):
        rows = slice(i * tl, (i + 1) * tl)
        if r > 1:
            for c in range(D_MODEL // LANES):
                h_scr[rows, c * LANES:(c + 1) * LANES] = hs_scr[
                    c, pl.ds(i, tl, stride=r), :].astype(BF16)
        if rotary:
            tab_scr[0, rows, :] = _strided_rows(c_ref, i, tl, r)
            tab_scr[1, rows, :] = _strided_rows(s_ref, i, tl, r)
    if with_f:
        f = jnp.dot(h_scr[...], wf_ref[...], preferred_element_type=F32)
        ft_ref[0] = f.T

    def main(idx, cc):
        return jnp.dot(h_scr[...], w_refs[idx][:, cc * MXU_DIM:(cc + 1) * MXU_DIM],
                       preferred_element_type=F32)

    chunks = [(idx, cc) for idx in range(len(kinds)) for cc in range(nchunk)]
    nxt = main(*chunks[0])
    for n, (idx, cc) in enumerate(chunks):
        cols = slice(cc * MXU_DIM, (cc + 1) * MXU_DIM)
        a = nxt
        if n + 1 < len(chunks):
            nxt = main(*chunks[n + 1])
        if kinds[idx] == "norm":
            parts = []
            for lt in range(reps):
                y = a[:, lt * LANES:(lt + 1) * LANES]
                y = y * y
                lowm = _lane_is_low(y.shape)
                s_lo = jnp.sum(jnp.where(lowm, y, 0.0), axis=-1, keepdims=True)
                s_hi = jnp.sum(jnp.where(lowm, 0.0, y), axis=-1, keepdims=True)
                parts.append(jnp.where(lowm, s_lo, s_hi) * (1.0 / HEAD_DIM))
            rs = lax.rsqrt(jnp.concatenate(parts, axis=1) + EPS)
            if rotary:
                u = a * gain_ref[idx, :, cols]
                up = jnp.dot(u.astype(BF16), p_ref[...], preferred_element_type=F32)
                cosv = jnp.concatenate([tab_scr[0]] * reps, axis=1)
                sinv = jnp.concatenate([tab_scr[1]] * reps, axis=1)
                a = (u * cosv + up * sinv) * rs
            else:
                a = a * rs * gain_ref[idx, :, cols]
        ab = a.astype(BF16)
        for i in range(r):
            for lt in range(reps):
                out_ref[0, i, idx * N_TILES + cc * reps + lt] = ab[
                    i * tl:(i + 1) * tl, lt * LANES:(lt + 1) * LANES]


def _rotary_partner_matrix():
    half = ROT_DIM // 2
    i = np.arange(MXU_DIM)
    d = i % HEAD_DIM
    partner = np.where(d < half, i + half, np.where(d < ROT_DIM, i - half, -1))
    return jnp.asarray(i[:, None] == partner[None, :], BF16)


def _inproj(x, norm_g, w, lj, sec_idx, gains, kinds, *, r, tabs=None, w_f=None):
    B, S, _ = x.shape
    L = S // r
    tl = ROW_TILE // r
    nsec = len(kinds)
    rotary = tabs is not None
    with_f = w_f is not None

    row_map = lambda b, t: (b, t, 0)
    const2 = lambda b, t: (0, 0)
    const3 = lambda b, t: (0, 0, 0)
    in_specs = [
        pl.BlockSpec((1, ROW_TILE, D_MODEL), row_map),
        pl.BlockSpec((1, D_MODEL), const2),
        pl.BlockSpec((nsec, 1, INNER), const3),
    ]
    args = [x, norm_g.reshape(1, D_MODEL), gains]
    for s in sec_idx:
        in_specs.append(pl.BlockSpec((None, D_MODEL, INNER), lambda b, t, s=s: (lj, 0, s)))
        args.append(w)
    scratch = [pltpu.VMEM((ROW_TILE, D_MODEL), BF16)]
    if rotary:
        for t in tabs:
            in_specs.append(pl.BlockSpec((1, ROW_TILE, LANES), row_map))
            args.append(t)
        in_specs.append(pl.BlockSpec((MXU_DIM, MXU_DIM), const2))
        args.append(_rotary_partner_matrix())
        scratch.append(pltpu.VMEM((2, ROW_TILE, LANES), F32))
    if r > 1:
        scratch.append(pltpu.VMEM((D_MODEL // LANES, ROW_TILE, LANES), F32))
    if with_f:
        in_specs.append(pl.BlockSpec((None, D_MODEL, LANES), lambda b, t: (lj, 0, 0)))
        args.append(w_f)
    out_shape = [jax.ShapeDtypeStruct((B, r, nsec * N_TILES, L, LANES), BF16)]
    out_specs = [pl.BlockSpec((1, r, nsec * N_TILES, tl, LANES), lambda b, t: (b, 0, 0, t, 0))]
    if with_f:
        out_shape.append(jax.ShapeDtypeStruct((B, LANES, S), F32))
        out_specs.append(pl.BlockSpec((1, LANES, ROW_TILE), lambda b, t: (b, 0, t)))
    return pl.pallas_call(
        functools.partial(_inproj_kernel, r=r, kinds=kinds, rotary=rotary, with_f=with_f),
        grid=(B, S // ROW_TILE),
        in_specs=in_specs,
        out_specs=out_specs,
        out_shape=out_shape,
        scratch_shapes=scratch,
        compiler_params=pltpu.CompilerParams(
            dimension_semantics=("arbitrary",) * 2, vmem_limit_bytes=VMEM_LIMIT),
        name=f"inproj_r{r}",
    )(*args)


def _fox_c_kernel(ft_ref, bf_ref, c_ref, *, blk):
    x = ft_ref[0] + bf_ref[...]
    ls = jnp.minimum(x, 0.0) - jnp.log1p(jnp.exp(-jnp.abs(x)))
    S = ls.shape[-1]
    tri = (lax.broadcasted_iota(jnp.int32, (blk, blk), 0)
           <= lax.broadcasted_iota(jnp.int32, (blk, blk), 1)).astype(BF16)
    carry = jnp.zeros((N_HEADS, 1), F32)
    for n in range(S // blk):
        t = ls[:, n * blk:(n + 1) * blk]
        h1 = t.astype(BF16)
        r1 = t - h1.astype(F32)
        h2 = r1.astype(BF16)
        h3 = (r1 - h2.astype(F32)).astype(BF16)
        cs = (jnp.dot(h1, tri, preferred_element_type=F32)
              + jnp.dot(h2, tri, preferred_element_type=F32)
              + jnp.dot(h3, tri, preferred_element_type=F32)) + carry
        c_ref[0, :, n * blk:(n + 1) * blk] = cs
        carry = cs[:, blk - 1:blk]


def _fox_c(ft, b_f):
    B, _, S = ft.shape
    return pl.pallas_call(
        functools.partial(_fox_c_kernel, blk=FOX_SUB),
        grid=(B,),
        in_specs=[pl.BlockSpec((1, N_HEADS, S), lambda b: (b, 0, 0)),
                  pl.BlockSpec((N_HEADS, 1), lambda b: (0, 0))],
        out_specs=pl.BlockSpec((1, N_HEADS, S), lambda b: (b, 0, 0)),
        out_shape=jax.ShapeDtypeStruct((B, N_HEADS, S), F32),
        compiler_params=pltpu.CompilerParams(dimension_semantics=("arbitrary",)),
        name="fox_c",
    )(ft, b_f.reshape(N_HEADS, 1))


def _fox_attn_kernel(q_ref, k_ref, v_ref, g_ref, c_ref, o_ref, m_scr, acc_scr, va_scr,
                     *, tq, sub):
    hp = pl.program_id(1)
    S = k_ref.shape[3]
    nsub = S // sub
    per_tile = tq // sub
    low = _lane_is_low((tq, LANES))

    for n in range(S // tq):
        rows = slice(n * tq, (n + 1) * tq)
        vf = v_ref[0, 0, 0, rows, :].astype(F32)
        va_scr[0, rows, :] = jnp.where(low, vf, 1.0).astype(BF16)
        va_scr[1, rows, :] = jnp.where(low, 1.0, vf).astype(BF16)

    mask = (lax.broadcasted_iota(jnp.int32, (tq, sub), 1)
            <= lax.broadcasted_iota(jnp.int32, (tq, sub), 0))

    def run(nq):
        qrows = slice(nq * tq, (nq + 1) * tq)
        qf = q_ref[0, 0, 0, qrows, :].astype(F32)
        qm = (jnp.where(low, qf, 0.0).astype(BF16), jnp.where(low, 0.0, qf).astype(BF16))
        m_scr[nq] = jnp.full(m_scr.shape[1:], NEG, F32)
        acc_scr[nq] = jnp.zeros(acc_scr.shape[1:], F32)

        def bias_row(hh, blk0, nblk):
            crow = (hp * HEADS_PER_TILE + hh) * nsub
            c0 = c_ref[0, crow + nq * per_tile][:, 0:1]
            cs = jnp.concatenate([c_ref[0, crow + blk0 + n] for n in range(nblk)], axis=1)
            return (c0 - cs) * LOG2E

        def scores(r0, blk0, nblk):
            kt = k_ref[0, 0, 0, blk0 * sub:(blk0 + nblk) * sub, :]
            return tuple(
                lax.dot_general(qm[hh][r0:], kt, (((1,), (1,)), ((), ())),
                                preferred_element_type=F32) + bias_row(hh, blk0, nblk)
                for hh in range(HEADS_PER_TILE))

        def accumulate(s, hh, r0, nr, k0, w):
            rows = slice(r0, r0 + nr)
            m_prev = m_scr[nq, hh, rows, :]
            m_new = jnp.maximum(m_prev, jnp.max(s, axis=-1, keepdims=True))
            alpha = jnp.exp2(m_prev - m_new)
            p = jnp.exp2(s - jnp.concatenate([m_new] * (w // LANES), axis=1))
            acc_scr[nq, hh, rows, :] = alpha * acc_scr[nq, hh, rows, :] + jnp.dot(
                p.astype(BF16), va_scr[hh, k0:k0 + w, :], preferred_element_type=F32)
            m_scr[nq, hh, rows, :] = m_new

        steps = [(0, j * per_tile, per_tile, None) for j in range(nq)]
        steps += [(0, nq * per_tile, 1, mask), (sub, nq * per_tile + 1, 1, mask[:sub])]
        s_cur = scores(*steps[0][:3])
        for i, (r0, blk0, nblk, msk) in enumerate(steps):
            if i + 1 < len(steps):
                s_next = scores(*steps[i + 1][:3])
            for hh in range(HEADS_PER_TILE):
                s = s_cur[hh] if msk is None else jnp.where(msk, s_cur[hh], NEG)
                accumulate(s, hh, r0, tq - r0, blk0 * sub, nblk * sub)
            s_cur = s_next

        num = jnp.where(low, acc_scr[nq, 0], acc_scr[nq, 1])
        den = pltpu.roll(jnp.where(low, acc_scr[nq, 1], acc_scr[nq, 0]), HEAD_DIM, 1)
        gate = g_ref[0, 0, 0, qrows, :].astype(F32)
        o_ref[0, 0, qrows, :] = (num * gate / (den * (1.0 + jnp.exp(-gate)))).astype(BF16)

    for nq in range(S // tq):
        run(nq)


def _fox_attn(qkvg, c):
    B, _, _, S, _ = qkvg.shape
    tq, sub = FOX_TQ, FOX_SUB
    assert tq == 2 * sub
    nrow = N_HEADS * (S // sub)
    c2 = c.reshape(B, nrow, 1, sub)
    return pl.pallas_call(
        functools.partial(_fox_attn_kernel, tq=tq, sub=sub),
        grid=(B, N_TILES),
        in_specs=[
            pl.BlockSpec((1, 1, 1, S, LANES), lambda b, hp, s=s: (b, 0, s * N_TILES + hp, 0, 0))
            for s in range(4)
        ] + [pl.BlockSpec((1, nrow, 1, sub), lambda b, hp: (b, 0, 0, 0))],
        out_specs=pl.BlockSpec((1, 1, S, LANES), lambda b, hp: (b, hp, 0, 0)),
        out_shape=jax.ShapeDtypeStruct((B, N_TILES, S, LANES), BF16),
        scratch_shapes=[pltpu.VMEM((S // tq, HEADS_PER_TILE, tq, LANES), F32),
                        pltpu.VMEM((S // tq, HEADS_PER_TILE, tq, LANES), F32),
                        pltpu.VMEM((HEADS_PER_TILE, S, LANES), BF16)],
        compiler_params=pltpu.CompilerParams(
            dimension_semantics=("arbitrary",) * 2, vmem_limit_bytes=VMEM_LIMIT),
        name="fox_attn",
    )(qkvg, qkvg, qkvg, qkvg, c2)


def _dsw_attn_kernel(*refs):
    (q0, k0, v0, gate_ref, q1, k1, v1, q2, k2, v2, o_ref, og_scr, lse_scr, *vx_scr) = refs
    groups = ((q0, k0, v0), (q1, k1, v1), (q2, k2, v2))
    S = o_ref.shape[2]
    low = _lane_is_low((BLOCK, LANES))
    row = lax.broadcasted_iota(jnp.int32, (BLOCK, 2 * BLOCK), 0)
    col = lax.broadcasted_iota(jnp.int32, (BLOCK, 2 * BLOCK), 1)
    band = (col >= row) & (col <= row + BLOCK)
    band0 = band & (col >= BLOCK)
    band2 = jnp.concatenate([band] * HEADS_PER_TILE, axis=0)
    band02 = jnp.concatenate([band0] * HEADS_PER_TILE, axis=0)

    def band_slices(n):
        cur = slice(n * BLOCK, (n + 1) * BLOCK)
        prev = slice(max(n - 1, 0) * BLOCK, (max(n - 1, 0) + 1) * BLOCK)
        return prev, cur

    one_lo = jnp.where(low, 1.0, 0.0).astype(BF16)
    one_hi = jnp.where(low, 0.0, 1.0).astype(BF16)
    for g, (_, r) in enumerate(DSWA_GROUPS):
        v_ref = groups[g][2]
        for z in range(r):
            for n in range(S // r // BLOCK):
                cur = slice(n * BLOCK, (n + 1) * BLOCK)
                vf = v_ref[0, z, 0, cur, :].astype(F32)
                vx_scr[g][0, z, cur, :LANES] = jnp.where(low, vf, 0.0).astype(BF16)
                vx_scr[g][0, z, cur, LANES:] = one_lo
                vx_scr[g][1, z, cur, :LANES] = jnp.where(low, 0.0, vf).astype(BF16)
                vx_scr[g][1, z, cur, LANES:] = one_hi

    def scores(g, z, n):
        q_ref, k_ref, _ = groups[g]
        prev, cur = band_slices(n)
        qf = q_ref[0, z, 0, cur, :].astype(F32)
        kb = jnp.concatenate([k_ref[0, z, 0, prev, :], k_ref[0, z, 0, cur, :]], axis=0)
        valid = band2 if n > 0 else band02
        qm = jnp.concatenate([jnp.where(low, qf, 0.0), jnp.where(low, 0.0, qf)],
                             axis=0).astype(BF16)
        s = lax.dot_general(qm, kb, (((1,), (1,)), ((), ())), preferred_element_type=F32)
        return jnp.where(valid, s, NEG)

    def finish(s, g, z, n):
        r = DSWA_GROUPS[g][1]
        prev, cur = band_slices(n)
        m = jnp.max(s, axis=-1, keepdims=True)
        p = jnp.exp2(s - m).astype(BF16)
        vx = jnp.concatenate([vx_scr[g][hh, z, sl, :] for hh in range(HEADS_PER_TILE)
                              for sl in (prev, cur)], axis=0)
        ol = jnp.dot(jnp.concatenate([p[:BLOCK], p[BLOCK:]], axis=1), vx,
                     preferred_element_type=F32)
        l = ol[:, LANES:]
        dst = pl.ds(z + r * n * BLOCK, BLOCK, stride=r) if r > 1 else cur
        og_scr[g, dst, :] = ol[:, :LANES] / l
        lse_scr[g, dst, :] = jnp.where(low, m[:BLOCK], m[BLOCK:]) + jnp.log2(l)

    for g, (_, r) in enumerate(DSWA_GROUPS):
        for z in range(r):
            for n in range(S // r // BLOCK):
                finish(scores(g, z, n), g, z, n)

    chunk = 2 * BLOCK

    def combine(i, carry):
        rows = pl.ds(pl.multiple_of(i * chunk, chunk), chunk)
        l0, l1, l2 = lse_scr[0, rows, :], lse_scr[1, rows, :], lse_scr[2, rows, :]
        mx = jnp.maximum(jnp.maximum(l0, l1), l2)
        e0, e1, e2 = jnp.exp2(l0 - mx), jnp.exp2(l1 - mx), jnp.exp2(l2 - mx)
        num = e0 * og_scr[0, rows, :] + e1 * og_scr[1, rows, :] + e2 * og_scr[2, rows, :]
        gate = gate_ref[0, 0, 0, rows, :].astype(F32)
        den = (e0 + e1 + e2) * (1.0 + jnp.exp(-gate))
        o_ref[0, 0, rows, :] = (num * gate / den).astype(BF16)
        return carry

    lax.fori_loop(0, S // chunk, combine, 0)


def _dsw_attn(p0, p1, p2):
    B, _, _, S, _ = p0.shape
    in_specs, args = [], []
    for g, (p, (_, r)) in enumerate(zip((p0, p1, p2), DSWA_GROUPS)):
        for sec in range(3):
            in_specs.append(pl.BlockSpec(
                (1, r, 1, S // r, LANES),
                lambda b, hp, sec=sec: (b, 0, sec * N_TILES + hp, 0, 0)))
            args.append(p)
        if g == 0:
            in_specs.append(pl.BlockSpec(
                (1, 1, 1, S, LANES), lambda b, hp: (b, 0, 3 * N_TILES + hp, 0, 0)))
            args.append(p)
    return pl.pallas_call(
        _dsw_attn_kernel,
        grid=(B, N_TILES),
        in_specs=in_specs,
        out_specs=pl.BlockSpec((1, 1, S, LANES), lambda b, hp: (b, hp, 0, 0)),
        out_shape=jax.ShapeDtypeStruct((B, N_TILES, S, LANES), BF16),
        scratch_shapes=[pltpu.VMEM((len(DSWA_GROUPS), S, LANES), F32),
                        pltpu.VMEM((len(DSWA_GROUPS), S, LANES), F32)] + [
            pltpu.VMEM((HEADS_PER_TILE, r, S // r, 2 * LANES), BF16) for _, r in DSWA_GROUPS],
        compiler_params=pltpu.CompilerParams(
            dimension_semantics=("arbitrary",) * 2, vmem_limit_bytes=VMEM_LIMIT),
        name="dsw_attn",
    )(*args)


def _outproj_kernel(a_ref, w_ref, x_ref, o_ref):
    a = jnp.concatenate([a_ref[0, c] for c in range(N_TILES)], axis=1)
    o_ref[0] = x_ref[0] + jnp.dot(a, w_ref[...], preferred_element_type=F32)


def _outproj(a, w, lj, x):
    B, S, _ = x.shape
    return pl.pallas_call(
        _outproj_kernel,
        grid=(B, S // OUT_ROW_TILE),
        in_specs=[pl.BlockSpec((1, N_TILES, OUT_ROW_TILE, LANES), lambda b, t: (b, 0, t, 0)),
                  pl.BlockSpec((None, INNER, D_MODEL), lambda b, t: (lj, 0, 0)),
                  pl.BlockSpec((1, OUT_ROW_TILE, D_MODEL), lambda b, t: (b, t, 0))],
        out_specs=pl.BlockSpec((1, OUT_ROW_TILE, D_MODEL), lambda b, t: (b, t, 0)),
        out_shape=jax.ShapeDtypeStruct((B, S, D_MODEL), F32),
        compiler_params=pltpu.CompilerParams(
            dimension_semantics=("arbitrary",) * 2, vmem_limit_bytes=VMEM_LIMIT),
        name="outproj",
    )(a, w, x)


def _gain_rows(q_g, k_g, n_plain):
    rows = [jnp.tile(q_g * (SCALE * LOG2E), N_HEADS), jnp.tile(k_g, N_HEADS)]
    rows += [jnp.ones((INNER,), F32)] * n_plain
    return jnp.stack(rows, axis=0).reshape(len(rows), 1, INNER).astype(F32)


def _rotary_tables(positions):
    half = ROT_DIM // 2
    inv_freq = ROPE_THETA ** (-jnp.arange(0, ROT_DIM, 2, dtype=F32) / ROT_DIM)
    ang = positions.astype(F32)[..., None] * inv_freq
    cos, sin = jnp.cos(ang), jnp.sin(ang)
    B, S = positions.shape
    one = jnp.ones((B, S, HEAD_DIM - ROT_DIM), F32)
    zero = jnp.zeros((B, S, HEAD_DIM - ROT_DIM), F32)
    c = jnp.concatenate([cos, cos, one], axis=-1)
    s = jnp.concatenate([-sin, sin, zero], axis=-1)
    return tuple(jnp.tile(t, (1, 1, HEADS_PER_TILE)) for t in (c, s))


def kernel(x, positions, norm_g, fox_w_in, fox_b_f, fox_q_norm, fox_k_norm, fox_w_out,
           dsw_w_in, dsw_q_norm, dsw_k_norm, dsw_w_out):
    B, S, _ = x.shape
    depth = norm_g.shape[0]
    tabs = _rotary_tables(positions)
    fox_w = fox_w_in.astype(BF16)
    fox_wf = jnp.pad(fox_w[:, :, 4 * INNER:], ((0, 0), (0, 0), (0, LANES - N_HEADS)))
    dsw_w = dsw_w_in.astype(BF16)
    fox_wo = fox_w_out.astype(BF16)
    dsw_wo = dsw_w_out.astype(BF16)
    for i in range(depth):
        j = i // 2
        if i % 2 == 0:
            gains = _gain_rows(fox_q_norm[j], fox_k_norm[j], 2)
            qkvg, ft = _inproj(x, norm_g[i], fox_w, j, (0, 1, 2, 3), gains,
                               ("norm", "norm", "plain", "plain"), r=1, w_f=fox_wf)
            c = _fox_c(ft, fox_b_f[j])
            a = _fox_attn(qkvg, c)
            x = _outproj(a, fox_wo, j, x)
        else:
            projs = []
            n_groups = len(DSWA_GROUPS)
            for g, (_, r) in enumerate(DSWA_GROUPS):
                idxs = (g, n_groups + g, 2 * n_groups + g) + ((3 * n_groups,) if g == 0 else ())
                kinds = ("norm", "norm", "plain") + (("plain",) if g == 0 else ())
                gains = _gain_rows(dsw_q_norm[j, g], dsw_k_norm[j, g], len(kinds) - 2)
                projs.append(_inproj(x, norm_g[i], dsw_w, j, idxs, gains, kinds,
                                     r=r, tabs=tabs)[0])
            a = _dsw_attn(*projs)
            x = _outproj(a, dsw_wo, j, x)
    return x
```

```python
import functools

import jax
import jax.numpy as jnp
import numpy as np
from jax import lax
from jax.experimental import pallas as pl
from jax.experimental.pallas import tpu as pltpu

D_MODEL = 1024
HEAD_DIM = 64
N_HEADS = 16
INNER = N_HEADS * HEAD_DIM
ROT_DIM = HEAD_DIM // 4
ROPE_THETA = 500000.0
DSWA_GROUPS = ((128, 1), (512, 4), (2048, 16))
BLOCK = 128
EPS = 1e-6
SCALE = HEAD_DIM ** -0.5
LOG2E = 1.4426950408889634

LANES = 128
MXU_DIM = 256
HEADS_PER_TILE = LANES // HEAD_DIM
N_TILES = INNER // LANES
NEG = -1e30
VMEM_LIMIT = 58 * 1024 * 1024

ROW_TILE = 1024
OUT_ROW_TILE = 1024
FOX_TQ = 512
FOX_SUB = 256

F32 = jnp.float32
BF16 = jnp.bfloat16


def _lane_is_low(shape):
    return lax.broadcasted_iota(jnp.int32, shape, len(shape) - 1) < HEAD_DIM


def _strided_rows(ref, i, n, stride):
    if stride == 1:
        return ref[0, i * n:(i + 1) * n, :]
    return ref[0, pl.ds(i, n, stride=stride), :]


def _inproj_kernel(*refs, r, kinds, rotary, with_f):
    it = iter(refs)
    x_ref, ng_ref, gain_ref = (next(it) for _ in range(3))
    w_refs = [next(it) for _ in kinds]
    if rotary:
        c_ref, s_ref, p_ref = next(it), next(it), next(it)
    if with_f:
        wf_ref = next(it)
    out_ref = next(it)
    if with_f:
        ft_ref = next(it)
    h_scr = next(it)
    if rotary:
        tab_scr = next(it)
    if r > 1:
        hs_scr = next(it)

    tl = ROW_TILE // r
    reps = MXU_DIM // LANES
    nchunk = INNER // MXU_DIM

    x = x_ref[0]
    ms = jnp.mean(x * x, axis=-1, keepdims=True)
    h = x * lax.rsqrt(ms + EPS) * ng_ref[...]
    if r == 1:
        h_scr[...] = h.astype(BF16)
    else:
        for c in range(D_MODEL // LANES):
            hs_scr[c] = h[:, c * LANES:(c + 1) * LANES]
    for i in range(r):
        rows = slice(i * tl, (i + 1) * tl)
        if r > 1:
            for c in range(D_MODEL // LANES):
                h_scr[rows, c * LANES:(c + 1) * LANES] = hs_scr[
                    c, pl.ds(i, tl, stride=r), :].astype(BF16)
        if rotary:
            tab_scr[0, rows, :] = _strided_rows(c_ref, i, tl, r)
            tab_scr[1, rows, :] = _strided_rows(s_ref, i, tl, r)
    if with_f:
        f = jnp.dot(h_scr[...], wf_ref[...], preferred_element_type=F32)
        ft_ref[0] = f.T

    def main(idx, cc):
        return jnp.dot(h_scr[...], w_refs[idx][:, cc * MXU_DIM:(cc + 1) * MXU_DIM],
                       preferred_element_type=F32)

    chunks = [(idx, cc) for idx in range(len(kinds)) for cc in range(nchunk)]
    nxt = main(*chunks[0])
    for n, (idx, cc) in enumerate(chunks):
        cols = slice(cc * MXU_DIM, (cc + 1) * MXU_DIM)
        a = nxt
        if n + 1 < len(chunks):
            nxt = main(*chunks[n + 1])
        if kinds[idx] == "norm":
            parts = []
            for lt in range(reps):
                y = a[:, lt * LANES:(lt + 1) * LANES]
                y = y * y
                lowm = _lane_is_low(y.shape)
                s_lo = jnp.sum(jnp.where(lowm, y, 0.0), axis=-1, keepdims=True)
                s_hi = jnp.sum(jnp.where(lowm, 0.0, y), axis=-1, keepdims=True)
                parts.append(jnp.where(lowm, s_lo, s_hi) * (1.0 / HEAD_DIM))
            rs = lax.rsqrt(jnp.concatenate(parts, axis=1) + EPS)
            if rotary:
                u = a * gain_ref[idx, :, cols]
                up = jnp.dot(u.astype(BF16), p_ref[...], preferred_element_type=F32)
                cosv = jnp.concatenate([tab_scr[0]] * reps, axis=1)
                sinv = jnp.concatenate([tab_scr[1]] * reps, axis=1)
                a = (u * cosv + up * sinv) * rs
            else:
                a = a * rs * gain_ref[idx, :, cols]
        ab = a.astype(BF16)
        for i in range(r):
            for lt in range(reps):
                out_ref[0, i, idx * N_TILES + cc * reps + lt] = ab[
                    i * tl:(i + 1) * tl, lt * LANES:(lt + 1) * LANES]


def _rotary_partner_matrix():
    half = ROT_DIM // 2
    i = np.arange(MXU_DIM)
    d = i % HEAD_DIM
    partner = np.where(d < half, i + half, np.where(d < ROT_DIM, i - half, -1))
    return jnp.asarray(i[:, None] == partner[None, :], BF16)


def _inproj(x, norm_g, w, lj, sec_idx, gains, kinds, *, r, tabs=None, w_f=None):
    B, S, _ = x.shape
    L = S // r
    tl = ROW_TILE // r
    nsec = len(kinds)
    rotary = tabs is not None
    with_f = w_f is not None

    row_map = lambda b, t: (b, t, 0)
    const2 = lambda b, t: (0, 0)
    const3 = lambda b, t: (0, 0, 0)
    in_specs = [
        pl.BlockSpec((1, ROW_TILE, D_MODEL), row_map),
        pl.BlockSpec((1, D_MODEL), const2),
        pl.BlockSpec((nsec, 1, INNER), const3),
    ]
    args = [x, norm_g.reshape(1, D_MODEL), gains]
    for s in sec_idx:
        in_specs.append(pl.BlockSpec((None, D_MODEL, INNER), lambda b, t, s=s: (lj, 0, s)))
        args.append(w)
    scratch = [pltpu.VMEM((ROW_TILE, D_MODEL), BF16)]
    if rotary:
        for t in tabs:
            in_specs.append(pl.BlockSpec((1, ROW_TILE, LANES), row_map))
            args.append(t)
        in_specs.append(pl.BlockSpec((MXU_DIM, MXU_DIM), const2))
        args.append(_rotary_partner_matrix())
        scratch.append(pltpu.VMEM((2, ROW_TILE, LANES), F32))
    if r > 1:
        scratch.append(pltpu.VMEM((D_MODEL // LANES, ROW_TILE, LANES), F32))
    if with_f:
        in_specs.append(pl.BlockSpec((None, D_MODEL, LANES), lambda b, t: (lj, 0, 0)))
        args.append(w_f)
    out_shape = [jax.ShapeDtypeStruct((B, r, nsec * N_TILES, L, LANES), BF16)]
    out_specs = [pl.BlockSpec((1, r, nsec * N_TILES, tl, LANES), lambda b, t: (b, 0, 0, t, 0))]
    if with_f:
        out_shape.append(jax.ShapeDtypeStruct((B, LANES, S), F32))
        out_specs.append(pl.BlockSpec((1, LANES, ROW_TILE), lambda b, t: (b, 0, t)))
    return pl.pallas_call(
        functools.partial(_inproj_kernel, r=r, kinds=kinds, rotary=rotary, with_f=with_f),
        grid=(B, S // ROW_TILE),
        in_specs=in_specs,
        out_specs=out_specs,
        out_shape=out_shape,
        scratch_shapes=scratch,
        compiler_params=pltpu.CompilerParams(
            dimension_semantics=("arbitrary",) * 2, vmem_limit_bytes=VMEM_LIMIT),
        name=f"inproj_r{r}",
    )(*args)


def _fox_c_kernel(ft_ref, bf_ref, c_ref, *, blk):
    x = ft_ref[0] + bf_ref[...]
    ls = jnp.minimum(x, 0.0) - jnp.log1p(jnp.exp(-jnp.abs(x)))
    S = ls.shape[-1]
    tri = (lax.broadcasted_iota(jnp.int32, (blk, blk), 0)
           <= lax.broadcasted_iota(jnp.int32, (blk, blk), 1)).astype(BF16)
    carry = jnp.zeros((N_HEADS, 1), F32)
    for n in range(S // blk):
        t = ls[:, n * blk:(n + 1) * blk]
        h1 = t.astype(BF16)
        r1 = t - h1.astype(F32)
        h2 = r1.astype(BF16)
        h3 = (r1 - h2.astype(F32)).astype(BF16)
        cs = (jnp.dot(h1, tri, preferred_element_type=F32)
              + jnp.dot(h2, tri, preferred_element_type=F32)
              + jnp.dot(h3, tri, preferred_element_type=F32)) + carry
        c_ref[0, :, n * blk:(n + 1) * blk] = cs
        carry = cs[:, blk - 1:blk]


def _fox_c(ft, b_f):
    B, _, S = ft.shape
    return pl.pallas_call(
        functools.partial(_fox_c_kernel, blk=FOX_SUB),
        grid=(B,),
        in_specs=[pl.BlockSpec((1, N_HEADS, S), lambda b: (b, 0, 0)),
                  pl.BlockSpec((N_HEADS, 1), lambda b: (0, 0))],
        out_specs=pl.BlockSpec((1, N_HEADS, S), lambda b: (b, 0, 0)),
        out_shape=jax.ShapeDtypeStruct((B, N_HEADS, S), F32),
        compiler_params=pltpu.CompilerParams(dimension_semantics=("arbitrary",)),
        name="fox_c",
    )(ft, b_f.reshape(N_HEADS, 1))


def _fox_attn_kernel(q_ref, k_ref, v_ref, g_ref, c_ref, o_ref, m_scr, acc_scr, va_scr,
                     *, tq, sub):
    hp = pl.program_id(1)
    S = k_ref.shape[3]
    nsub = S // sub
    per_tile = tq // sub
    low = _lane_is_low((tq, LANES))

    for n in range(S // tq):
        rows = slice(n * tq, (n + 1) * tq)
        vf = v_ref[0, 0, 0, rows, :].astype(F32)
        va_scr[0, rows, :] = jnp.where(low, vf, 1.0).astype(BF16)
        va_scr[1, rows, :] = jnp.where(low, 1.0, vf).astype(BF16)

    mask = (lax.broadcasted_iota(jnp.int32, (tq, sub), 1)
            <= lax.broadcasted_iota(jnp.int32, (tq, sub), 0))

    def run(nq):
        qrows = slice(nq * tq, (nq + 1) * tq)
        qf = q_ref[0, 0, 0, qrows, :].astype(F32)
        qm = (jnp.where(low, qf, 0.0).astype(BF16), jnp.where(low, 0.0, qf).astype(BF16))
        m_scr[nq] = jnp.full(m_scr.shape[1:], NEG, F32)
        acc_scr[nq] = jnp.zeros(acc_scr.shape[1:], F32)

        def bias_row(hh, blk0, nblk):
            crow = (hp * HEADS_PER_TILE + hh) * nsub
            c0 = c_ref[0, crow + nq * per_tile][:, 0:1]
            cs = jnp.concatenate([c_ref[0, crow + blk0 + n] for n in range(nblk)], axis=1)
            return (c0 - cs) * LOG2E

        def scores(r0, blk0, nblk):
            kt = k_ref[0, 0, 0, blk0 * sub:(blk0 + nblk) * sub, :]
            return tuple(
                lax.dot_general(qm[hh][r0:], kt, (((1,), (1,)), ((), ())),
                                preferred_element_type=F32) + bias_row(hh, blk0, nblk)
                for hh in range(HEADS_PER_TILE))

        def accumulate(s, hh, r0, nr, k0, w):
            rows = slice(r0, r0 + nr)
            m_prev = m_scr[nq, hh, rows, :]
            m_new = jnp.maximum(m_prev, jnp.max(s, axis=-1, keepdims=True))
            alpha = jnp.exp2(m_prev - m_new)
            p = jnp.exp2(s - jnp.concatenate([m_new] * (w // LANES), axis=1))
            acc_scr[nq, hh, rows, :] = alpha * acc_scr[nq, hh, rows, :] + jnp.dot(
                p.astype(BF16), va_scr[hh, k0:k0 + w, :], preferred_element_type=F32)
            m_scr[nq, hh, rows, :] = m_new

        steps = [(0, j * per_tile, per_tile, None) for j in range(nq)]
        steps += [(0, nq * per_tile, 1, mask), (sub, nq * per_tile + 1, 1, mask[:sub])]
        s_cur = scores(*steps[0][:3])
        for i, (r0, blk0, nblk, msk) in enumerate(steps):
            if i + 1 < len(steps):
                s_next = scores(*steps[i + 1][:3])
            for hh in range(HEADS_PER_TILE):
                s = s_cur[hh] if msk is None else jnp.where(msk, s_cur[hh], NEG)
                accumulate(s, hh, r0, tq - r0, blk0 * sub, nblk * sub)
            s_cur = s_next

        num = jnp.where(low, acc_scr[nq, 0], acc_scr[nq, 1])
        den = pltpu.roll(jnp.where(low, acc_scr[nq, 1], acc_scr[nq, 0]), HEAD_DIM, 1)
        gate = g_ref[0, 0, 0, qrows, :].astype(F32)
        o_ref[0, 0, qrows, :] = (num * gate / (den * (1.0 + jnp.exp(-gate)))).astype(BF16)

    for nq in range(S // tq):
        run(nq)


def _fox_attn(qkvg, c):
    B, _, _, S, _ = qkvg.shape
    tq, sub = FOX_TQ, FOX_SUB
    assert tq == 2 * sub
    nrow = N_HEADS * (S // sub)
    c2 = c.reshape(B, nrow, 1, sub)
    return pl.pallas_call(
        functools.partial(_fox_attn_kernel, tq=tq, sub=sub),
        grid=(B, N_TILES),
        in_specs=[
            pl.BlockSpec((1, 1, 1, S, LANES), lambda b, hp, s=s: (b, 0, s * N_TILES + hp, 0, 0))
            for s in range(4)
        ] + [pl.BlockSpec((1, nrow, 1, sub), lambda b, hp: (b, 0, 0, 0))],
        out_specs=pl.BlockSpec((1, 1, S, LANES), lambda b, hp: (b, hp, 0, 0)),
        out_shape=jax.ShapeDtypeStruct((B, N_TILES, S, LANES), BF16),
        scratch_shapes=[pltpu.VMEM((S // tq, HEADS_PER_TILE, tq, LANES), F32),
                        pltpu.VMEM((S // tq, HEADS_PER_TILE, tq, LANES), F32),
                        pltpu.VMEM((HEADS_PER_TILE, S, LANES), BF16)],
        compiler_params=pltpu.CompilerParams(
            dimension_semantics=("arbitrary",) * 2, vmem_limit_bytes=VMEM_LIMIT),
        name="fox_attn",
    )(qkvg, qkvg, qkvg, qkvg, c2)


def _dsw_attn_kernel(*refs):
    (q0, k0, v0, gate_ref, q1, k1, v1, q2, k2, v2, o_ref, og_scr, lse_scr, *vx_scr) = refs
    groups = ((q0, k0, v0), (q1, k1, v1), (q2, k2, v2))
    S = o_ref.shape[2]
    low = _lane_is_low((BLOCK, LANES))
    row = lax.broadcasted_iota(jnp.int32, (BLOCK, 2 * BLOCK), 0)
    col = lax.broadcasted_iota(jnp.int32, (BLOCK, 2 * BLOCK), 1)
    band = (col >= row) & (col <= row + BLOCK)
    band2 = jnp.concatenate([band] * HEADS_PER_TILE, axis=0)
    first = (lax.broadcasted_iota(jnp.int32, (BLOCK, BLOCK), 1)
             <= lax.broadcasted_iota(jnp.int32, (BLOCK, BLOCK), 0))
    first2 = jnp.concatenate([first] * HEADS_PER_TILE, axis=0)

    def key_slices(n):
        cur = slice(n * BLOCK, (n + 1) * BLOCK)
        if n == 0:
            return (cur,)
        return slice((n - 1) * BLOCK, n * BLOCK), cur

    one_lo = jnp.where(low, 1.0, 0.0).astype(BF16)
    one_hi = jnp.where(low, 0.0, 1.0).astype(BF16)
    for g, (_, r) in enumerate(DSWA_GROUPS):
        v_ref = groups[g][2]
        for z in range(r):
            for n in range(S // r // BLOCK):
                cur = slice(n * BLOCK, (n + 1) * BLOCK)
                vf = v_ref[0, z, 0, cur, :].astype(F32)
                vx_scr[g][0, z, cur, :LANES] = jnp.where(low, vf, 0.0).astype(BF16)
                vx_scr[g][0, z, cur, LANES:] = one_lo
                vx_scr[g][1, z, cur, :LANES] = jnp.where(low, 0.0, vf).astype(BF16)
                vx_scr[g][1, z, cur, LANES:] = one_hi

    def scores(g, z, n):
        q_ref, k_ref, _ = groups[g]
        keys = key_slices(n)
        qf = q_ref[0, z, 0, keys[-1], :].astype(F32)
        kb = jnp.concatenate([k_ref[0, z, 0, sl, :] for sl in keys], axis=0)
        valid = band2 if n > 0 else first2
        qm = jnp.concatenate([jnp.where(low, qf, 0.0), jnp.where(low, 0.0, qf)],
                             axis=0).astype(BF16)
        s = lax.dot_general(qm, kb, (((1,), (1,)), ((), ())), preferred_element_type=F32)
        return jnp.where(valid, s, NEG)

    def finish(s, g, z, n):
        r = DSWA_GROUPS[g][1]
        keys = key_slices(n)
        cur = keys[-1]
        m = jnp.max(s, axis=-1, keepdims=True)
        p = jnp.exp2(s - m).astype(BF16)
        vx = jnp.concatenate([vx_scr[g][hh, z, sl, :] for hh in range(HEADS_PER_TILE)
                              for sl in keys], axis=0)
        ol = jnp.dot(jnp.concatenate([p[:BLOCK], p[BLOCK:]], axis=1), vx,
                     preferred_element_type=F32)
        l = ol[:, LANES:]
        dst = pl.ds(z + r * n * BLOCK, BLOCK, stride=r) if r > 1 else cur
        og_scr[g, dst, :] = ol[:, :LANES] / l
        lse_scr[g, dst, :] = jnp.where(low, m[:BLOCK], m[BLOCK:]) + jnp.log2(l)

    for g, (_, r) in enumerate(DSWA_GROUPS):
        for z in range(r):
            for n in range(S // r // BLOCK):
                finish(scores(g, z, n), g, z, n)

    chunk = 2 * BLOCK

    def combine(i, carry):
        rows = pl.ds(pl.multiple_of(i * chunk, chunk), chunk)
        l0, l1, l2 = lse_scr[0, rows, :], lse_scr[1, rows, :], lse_scr[2, rows, :]
        mx = jnp.maximum(jnp.maximum(l0, l1), l2)
        e0, e1, e2 = jnp.exp2(l0 - mx), jnp.exp2(l1 - mx), jnp.exp2(l2 - mx)
        num = e0 * og_scr[0, rows, :] + e1 * og_scr[1, rows, :] + e2 * og_scr[2, rows, :]
        gate = gate_ref[0, 0, 0, rows, :].astype(F32)
        den = (e0 + e1 + e2) * (1.0 + jnp.exp(-gate))
        o_ref[0, 0, rows, :] = (num * gate / den).astype(BF16)
        return carry

    lax.fori_loop(0, S // chunk, combine, 0)


def _dsw_attn(p0, p1, p2):
    B, _, _, S, _ = p0.shape
    in_specs, args = [], []
    for g, (p, (_, r)) in enumerate(zip((p0, p1, p2), DSWA_GROUPS)):
        for sec in range(3):
            in_specs.append(pl.BlockSpec(
                (1, r, 1, S // r, LANES),
                lambda b, hp, sec=sec: (b, 0, sec * N_TILES + hp, 0, 0)))
            args.append(p)
        if g == 0:
            in_specs.append(pl.BlockSpec(
                (1, 1, 1, S, LANES), lambda b, hp: (b, 0, 3 * N_TILES + hp, 0, 0)))
            args.append(p)
    return pl.pallas_call(
        _dsw_attn_kernel,
        grid=(B, N_TILES),
        in_specs=in_specs,
        out_specs=pl.BlockSpec((1, 1, S, LANES), lambda b, hp: (b, hp, 0, 0)),
        out_shape=jax.ShapeDtypeStruct((B, N_TILES, S, LANES), BF16),
        scratch_shapes=[pltpu.VMEM((len(DSWA_GROUPS), S, LANES), F32),
                        pltpu.VMEM((len(DSWA_GROUPS), S, LANES), F32)] + [
            pltpu.VMEM((HEADS_PER_TILE, r, S // r, 2 * LANES), BF16) for _, r in DSWA_GROUPS],
        compiler_params=pltpu.CompilerParams(
            dimension_semantics=("arbitrary",) * 2, vmem_limit_bytes=VMEM_LIMIT),
        name="dsw_attn",
    )(*args)


def _outproj_kernel(a_ref, w_ref, x_ref, o_ref):
    a = jnp.concatenate([a_ref[0, c] for c in range(N_TILES)], axis=1)
    o_ref[0] = x_ref[0] + jnp.dot(a, w_ref[...], preferred_element_type=F32)


def _outproj(a, w, lj, x):
    B, S, _ = x.shape
    return pl.pallas_call(
        _outproj_kernel,
        grid=(B, S // OUT_ROW_TILE),
        in_specs=[pl.BlockSpec((1, N_TILES, OUT_ROW_TILE, LANES), lambda b, t: (b, 0, t, 0)),
                  pl.BlockSpec((None, INNER, D_MODEL), lambda b, t: (lj, 0, 0)),
                  pl.BlockSpec((1, OUT_ROW_TILE, D_MODEL), lambda b, t: (b, t, 0))],
        out_specs=pl.BlockSpec((1, OUT_ROW_TILE, D_MODEL), lambda b, t: (b, t, 0)),
        out_shape=jax.ShapeDtypeStruct((B, S, D_MODEL), F32),
        compiler_params=pltpu.CompilerParams(
            dimension_semantics=("arbitrary",) * 2, vmem_limit_bytes=VMEM_LIMIT),
        name="outproj",
    )(a, w, x)


def _gain_rows(q_g, k_g, n_plain):
    rows = [jnp.tile(q_g * (SCALE * LOG2E), N_HEADS), jnp.tile(k_g, N_HEADS)]
    rows += [jnp.ones((INNER,), F32)] * n_plain
    return jnp.stack(rows, axis=0).reshape(len(rows), 1, INNER).astype(F32)


def _rotary_tables(positions):
    half = ROT_DIM // 2
    inv_freq = ROPE_THETA ** (-jnp.arange(0, ROT_DIM, 2, dtype=F32) / ROT_DIM)
    ang = positions.astype(F32)[..., None] * inv_freq
    cos, sin = jnp.cos(ang), jnp.sin(ang)
    B, S = positions.shape
    one = jnp.ones((B, S, HEAD_DIM - ROT_DIM), F32)
    zero = jnp.zeros((B, S, HEAD_DIM - ROT_DIM), F32)
    c = jnp.concatenate([cos, cos, one], axis=-1)
    s = jnp.concatenate([-sin, sin, zero], axis=-1)
    return tuple(jnp.tile(t, (1, 1, HEADS_PER_TILE)) for t in (c, s))


def kernel(x, positions, norm_g, fox_w_in, fox_b_f, fox_q_norm, fox_k_norm, fox_w_out,
           dsw_w_in, dsw_q_norm, dsw_k_norm, dsw_w_out):
    B, S, _ = x.shape
    depth = norm_g.shape[0]
    tabs = _rotary_tables(positions)
    fox_w = fox_w_in.astype(BF16)
    fox_wf = jnp.pad(fox_w[:, :, 4 * INNER:], ((0, 0), (0, 0), (0, LANES - N_HEADS)))
    dsw_w = dsw_w_in.astype(BF16)
    fox_wo = fox_w_out.astype(BF16)
    dsw_wo = dsw_w_out.astype(BF16)
    for i in range(depth):
        j = i // 2
        if i % 2 == 0:
            gains = _gain_rows(fox_q_norm[j], fox_k_norm[j], 2)
            qkvg, ft = _inproj(x, norm_g[i], fox_w, j, (0, 1, 2, 3), gains,
                               ("norm", "norm", "plain", "plain"), r=1, w_f=fox_wf)
            c = _fox_c(ft, fox_b_f[j])
            a = _fox_attn(qkvg, c)
            x = _outproj(a, fox_wo, j, x)
        else:
            projs = []
            n_groups = len(DSWA_GROUPS)
            for g, (_, r) in enumerate(DSWA_GROUPS):
                idxs = (g, n_groups + g, 2 * n_groups + g) + ((3 * n_groups,) if g == 0 else ())
                kinds = ("norm", "norm", "plain") + (("plain",) if g == 0 else ())
                gains = _gain_rows(dsw_q_norm[j, g], dsw_k_norm[j, g], len(kinds) - 2)
                projs.append(_inproj(x, norm_g[i], dsw_w, j, idxs, gains, kinds,
                                     r=r, tabs=tabs)[0])
            a = _dsw_attn(*projs)
            x = _outproj(a, dsw_wo, j, x)
    return x
```

```python
import functools

import jax
import jax.numpy as jnp
import numpy as np
from jax import lax
from jax.experimental import pallas as pl
from jax.experimental.pallas import tpu as pltpu

D_MODEL = 1024
HEAD_DIM = 64
N_HEADS = 16
INNER = N_HEADS * HEAD_DIM
ROT_DIM = HEAD_DIM // 4
ROPE_THETA = 500000.0
DSWA_GROUPS = ((128, 1), (512, 4), (2048, 16))
BLOCK = 128
EPS = 1e-6
SCALE = HEAD_DIM ** -0.5
LOG2E = 1.4426950408889634

LANES = 128
MXU_DIM = 256
HEADS_PER_TILE = LANES // HEAD_DIM
N_TILES = INNER // LANES
NEG = -1e30
VMEM_LIMIT = 58 * 1024 * 1024

ROW_TILE = 1024
OUT_ROW_TILE = 1024
FOX_TQ = 512
FOX_SUB = 256

F32 = jnp.float32
BF16 = jnp.bfloat16


def _lane_is_low(shape):
    return lax.broadcasted_iota(jnp.int32, shape, len(shape) - 1) < HEAD_DIM


def _strided_rows(ref, i, n, stride):
    if stride == 1:
        return ref[0, i * n:(i + 1) * n, :]
    return ref[0, pl.ds(i, n, stride=stride), :]


def _inproj_kernel(*refs, r, kinds, rotary, with_f):
    it = iter(refs)
    x_ref, ng_ref, gain_ref = (next(it) for _ in range(3))
    w_refs = [next(it) for _ in kinds]
    if rotary:
        c_ref, s_ref, p_ref = next(it), next(it), next(it)
    if with_f:
        wf_ref = next(it)
    out_ref = next(it)
    if with_f:
        ft_ref = next(it)
    h_scr = next(it)
    if rotary:
        tab_scr = next(it)
    if r > 1:
        hs_scr = next(it)

    tl = ROW_TILE // r
    reps = MXU_DIM // LANES
    nchunk = INNER // MXU_DIM

    x = x_ref[0]
    ms = jnp.mean(x * x, axis=-1, keepdims=True)
    h = x * lax.rsqrt(ms + EPS) * ng_ref[...]
    if r == 1:
        h_scr[...] = h.astype(BF16)
    else:
        for c in range(D_MODEL // LANES):
            hs_scr[c] = h[:, c * LANES:(c + 1) * LANES]
    for i in range(r):
        rows = slice(i * tl, (i + 1) * tl)
        if r > 1:
            for c in range(D_MODEL // LANES):
                h_scr[rows, c * LANES:(c + 1) * LANES] = hs_scr[
                    c, pl.ds(i, tl, stride=r), :].astype(BF16)
        if rotary:
            tab_scr[0, rows, :] = _strided_rows(c_ref, i, tl, r)
            tab_scr[1, rows, :] = _strided_rows(s_ref, i, tl, r)
    if with_f:
        f = jnp.dot(h_scr[...], wf_ref[...], preferred_element_type=F32)
        ft_ref[0] = f.T

    def main(idx, cc):
        return jnp.dot(h_scr[...], w_refs[idx][:, cc * MXU_DIM:(cc + 1) * MXU_DIM],
                       preferred_element_type=F32)

    if rotary:
        order = sorted(range(len(kinds)), key=lambda i: (kinds[i] == "plain", i))
        order = [order[i // 2 + (i % 2) * ((len(order) + 1) // 2)] for i in range(len(order))]
        chunks = [(idx, cc) for cc in range(nchunk) for idx in order]
    else:
        chunks = [(idx, cc) for idx in range(len(kinds)) for cc in range(nchunk)]
    nxt = main(*chunks[0])
    for n, (idx, cc) in enumerate(chunks):
        cols = slice(cc * MXU_DIM, (cc + 1) * MXU_DIM)
        a = nxt
        if n + 1 < len(chunks):
            nxt = main(*chunks[n + 1])
        if kinds[idx] == "norm":
            parts = []
            for lt in range(reps):
                y = a[:, lt * LANES:(lt + 1) * LANES]
                y = y * y
                lowm = _lane_is_low(y.shape)
                s_lo = jnp.sum(jnp.where(lowm, y, 0.0), axis=-1, keepdims=True)
                s_hi = jnp.sum(jnp.where(lowm, 0.0, y), axis=-1, keepdims=True)
                parts.append(jnp.where(lowm, s_lo, s_hi) * (1.0 / HEAD_DIM))
            rs = lax.rsqrt(jnp.concatenate(parts, axis=1) + EPS)
            if rotary:
                u = a * gain_ref[idx, :, cols]
                up = jnp.dot(u.astype(BF16), p_ref[...], preferred_element_type=F32)
                cosv = jnp.concatenate([tab_scr[0]] * reps, axis=1)
                sinv = jnp.concatenate([tab_scr[1]] * reps, axis=1)
                a = (u * cosv + up * sinv) * rs
            else:
                a = a * rs * gain_ref[idx, :, cols]
        ab = a.astype(BF16)
        for i in range(r):
            for lt in range(reps):
                out_ref[0, i, idx * N_TILES + cc * reps + lt] = ab[
                    i * tl:(i + 1) * tl, lt * LANES:(lt + 1) * LANES]


def _rotary_partner_matrix():
    half = ROT_DIM // 2
    i = np.arange(MXU_DIM)
    d = i % HEAD_DIM
    partner = np.where(d < half, i + half, np.where(d < ROT_DIM, i - half, -1))
    return jnp.asarray(i[:, None] == partner[None, :], BF16)


def _inproj(x, norm_g, w, lj, sec_idx, gains, kinds, *, r, tabs=None, w_f=None):
    B, S, _ = x.shape
    L = S // r
    tl = ROW_TILE // r
    nsec = len(kinds)
    rotary = tabs is not None
    with_f = w_f is not None

    row_map = lambda b, t: (b, t, 0)
    const2 = lambda b, t: (0, 0)
    const3 = lambda b, t: (0, 0, 0)
    in_specs = [
        pl.BlockSpec((1, ROW_TILE, D_MODEL), row_map),
        pl.BlockSpec((1, D_MODEL), const2),
        pl.BlockSpec((nsec, 1, INNER), const3),
    ]
    args = [x, norm_g.reshape(1, D_MODEL), gains]
    for s in sec_idx:
        in_specs.append(pl.BlockSpec((None, D_MODEL, INNER), lambda b, t, s=s: (lj, 0, s)))
        args.append(w)
    scratch = [pltpu.VMEM((ROW_TILE, D_MODEL), BF16)]
    if rotary:
        for t in tabs:
            in_specs.append(pl.BlockSpec((1, ROW_TILE, LANES), row_map))
            args.append(t)
        in_specs.append(pl.BlockSpec((MXU_DIM, MXU_DIM), const2))
        args.append(_rotary_partner_matrix())
        scratch.append(pltpu.VMEM((2, ROW_TILE, LANES), F32))
    if r > 1:
        scratch.append(pltpu.VMEM((D_MODEL // LANES, ROW_TILE, LANES), F32))
    if with_f:
        in_specs.append(pl.BlockSpec((None, D_MODEL, LANES), lambda b, t: (lj, 0, 0)))
        args.append(w_f)
    out_shape = [jax.ShapeDtypeStruct((B, r, nsec * N_TILES, L, LANES), BF16)]
    out_specs = [pl.BlockSpec((1, r, nsec * N_TILES, tl, LANES), lambda b, t: (b, 0, 0, t, 0))]
    if with_f:
        out_shape.append(jax.ShapeDtypeStruct((B, LANES, S), F32))
        out_specs.append(pl.BlockSpec((1, LANES, ROW_TILE), lambda b, t: (b, 0, t)))
    return pl.pallas_call(
        functools.partial(_inproj_kernel, r=r, kinds=kinds, rotary=rotary, with_f=with_f),
        grid=(B, S // ROW_TILE),
        in_specs=in_specs,
        out_specs=out_specs,
        out_shape=out_shape,
        scratch_shapes=scratch,
        compiler_params=pltpu.CompilerParams(
            dimension_semantics=("arbitrary",) * 2, vmem_limit_bytes=VMEM_LIMIT),
        name=f"inproj_r{r}",
    )(*args)


def _fox_c_kernel(ft_ref, bf_ref, c_ref, *, blk):
    x = ft_ref[0] + bf_ref[...]
    ls = jnp.minimum(x, 0.0) - jnp.log1p(jnp.exp(-jnp.abs(x)))
    S = ls.shape[-1]
    tri = (lax.broadcasted_iota(jnp.int32, (blk, blk), 0)
           <= lax.broadcasted_iota(jnp.int32, (blk, blk), 1)).astype(BF16)
    carry = jnp.zeros((N_HEADS, 1), F32)
    for n in range(S // blk):
        t = ls[:, n * blk:(n + 1) * blk]
        h1 = t.astype(BF16)
        r1 = t - h1.astype(F32)
        h2 = r1.astype(BF16)
        h3 = (r1 - h2.astype(F32)).astype(BF16)
        cs = (jnp.dot(h1, tri, preferred_element_type=F32)
              + jnp.dot(h2, tri, preferred_element_type=F32)
              + jnp.dot(h3, tri, preferred_element_type=F32)) + carry
        c_ref[0, :, n * blk:(n + 1) * blk] = cs
        carry = cs[:, blk - 1:blk]


def _fox_c(ft, b_f):
    B, _, S = ft.shape
    return pl.pallas_call(
        functools.partial(_fox_c_kernel, blk=FOX_SUB),
        grid=(B,),
        in_specs=[pl.BlockSpec((1, N_HEADS, S), lambda b: (b, 0, 0)),
                  pl.BlockSpec((N_HEADS, 1), lambda b: (0, 0))],
        out_specs=pl.BlockSpec((1, N_HEADS, S), lambda b: (b, 0, 0)),
        out_shape=jax.ShapeDtypeStruct((B, N_HEADS, S), F32),
        compiler_params=pltpu.CompilerParams(dimension_semantics=("arbitrary",)),
        name="fox_c",
    )(ft, b_f.reshape(N_HEADS, 1))


def _fox_attn_kernel(q_ref, k_ref, v_ref, g_ref, c_ref, o_ref, m_scr, acc_scr, va_scr,
                     *, tq, sub):
    hp = pl.program_id(1)
    S = k_ref.shape[3]
    nsub = S // sub
    per_tile = tq // sub
    low = _lane_is_low((tq, LANES))

    for n in range(S // tq):
        rows = slice(n * tq, (n + 1) * tq)
        vf = v_ref[0, 0, 0, rows, :].astype(F32)
        va_scr[0, rows, :] = jnp.where(low, vf, 1.0).astype(BF16)
        va_scr[1, rows, :] = jnp.where(low, 1.0, vf).astype(BF16)

    mask = (lax.broadcasted_iota(jnp.int32, (tq, sub), 1)
            <= lax.broadcasted_iota(jnp.int32, (tq, sub), 0))

    def run(nq):
        qrows = slice(nq * tq, (nq + 1) * tq)
        qf = q_ref[0, 0, 0, qrows, :].astype(F32)
        qm = (jnp.where(low, qf, 0.0).astype(BF16), jnp.where(low, 0.0, qf).astype(BF16))
        m_scr[nq] = jnp.full(m_scr.shape[1:], NEG, F32)
        acc_scr[nq] = jnp.zeros(acc_scr.shape[1:], F32)

        def bias_row(hh, blk0, nblk):
            crow = (hp * HEADS_PER_TILE + hh) * nsub
            c0 = c_ref[0, crow + nq * per_tile][:, 0:1]
            cs = jnp.concatenate([c_ref[0, crow + blk0 + n] for n in range(nblk)], axis=1)
            return (c0 - cs) * LOG2E

        def scores(hh, r0, blk0, nblk):
            kt = k_ref[0, 0, 0, blk0 * sub:(blk0 + nblk) * sub, :]
            return lax.dot_general(qm[hh][r0:], kt, (((1,), (1,)), ((), ())),
                                   preferred_element_type=F32) + bias_row(hh, blk0, nblk)

        def accumulate(s, hh, r0, nr, k0, w):
            rows = slice(r0, r0 + nr)
            m_prev = m_scr[nq, hh, rows, :]
            m_new = jnp.maximum(m_prev, jnp.max(s, axis=-1, keepdims=True))
            alpha = jnp.exp2(m_prev - m_new)
            p = jnp.exp2(s - jnp.concatenate([m_new] * (w // LANES), axis=1))
            acc_scr[nq, hh, rows, :] = alpha * acc_scr[nq, hh, rows, :] + jnp.dot(
                p.astype(BF16), va_scr[hh, k0:k0 + w, :], preferred_element_type=F32)
            m_scr[nq, hh, rows, :] = m_new

        steps = [(0, j * per_tile, per_tile, None) for j in range(nq)]
        steps += [(0, nq * per_tile, 1, mask), (sub, nq * per_tile + 1, 1, mask[:sub])]
        depth = 2
        queue = [[scores(hh, *steps[i][:3]) for hh in range(HEADS_PER_TILE)]
                 for i in range(min(depth, len(steps)))]
        for i, (r0, blk0, nblk, msk) in enumerate(steps):
            s_cur = queue.pop(0)
            if i + depth < len(steps):
                queue.append([scores(hh, *steps[i + depth][:3]) for hh in range(HEADS_PER_TILE)])
            for hh in range(HEADS_PER_TILE):
                s = s_cur[hh] if msk is None else jnp.where(msk, s_cur[hh], NEG)
                accumulate(s, hh, r0, tq - r0, blk0 * sub, nblk * sub)

        num = jnp.where(low, acc_scr[nq, 0], acc_scr[nq, 1])
        den = pltpu.roll(jnp.where(low, acc_scr[nq, 1], acc_scr[nq, 0]), HEAD_DIM, 1)
        gate = g_ref[0, 0, 0, qrows, :].astype(F32)
        o_ref[0, 0, qrows, :] = (num * gate / (den * (1.0 + jnp.exp(-gate)))).astype(BF16)

    for nq in reversed(range(S // tq)):
        run(nq)


def _fox_attn(qkvg, c):
    B, _, _, S, _ = qkvg.shape
    tq, sub = FOX_TQ, FOX_SUB
    assert tq == 2 * sub
    nrow = N_HEADS * (S // sub)
    c2 = c.reshape(B, nrow, 1, sub)
    return pl.pallas_call(
        functools.partial(_fox_attn_kernel, tq=tq, sub=sub),
        grid=(B, N_TILES),
        in_specs=[
            pl.BlockSpec((1, 1, 1, S, LANES), lambda b, hp, s=s: (b, 0, s * N_TILES + hp, 0, 0))
            for s in range(4)
        ] + [pl.BlockSpec((1, nrow, 1, sub), lambda b, hp: (b, 0, 0, 0))],
        out_specs=pl.BlockSpec((1, 1, S, LANES), lambda b, hp: (b, hp, 0, 0)),
        out_shape=jax.ShapeDtypeStruct((B, N_TILES, S, LANES), BF16),
        scratch_shapes=[pltpu.VMEM((S // tq, HEADS_PER_TILE, tq, LANES), F32),
                        pltpu.VMEM((S // tq, HEADS_PER_TILE, tq, LANES), F32),
                        pltpu.VMEM((HEADS_PER_TILE, S, LANES), BF16)],
        compiler_params=pltpu.CompilerParams(
            dimension_semantics=("arbitrary",) * 2, vmem_limit_bytes=VMEM_LIMIT),
        name="fox_attn",
    )(qkvg, qkvg, qkvg, qkvg, c2)


def _dsw_attn_kernel(*refs):
    (q0, k0, v0, gate_ref, q1, k1, v1, q2, k2, v2, o_ref, og_scr, lse_scr, *vx_scr) = refs
    groups = ((q0, k0, v0), (q1, k1, v1), (q2, k2, v2))
    S = o_ref.shape[2]
    low = _lane_is_low((BLOCK, LANES))
    row = lax.broadcasted_iota(jnp.int32, (BLOCK, 2 * BLOCK), 0)
    col = lax.broadcasted_iota(jnp.int32, (BLOCK, 2 * BLOCK), 1)
    band = (col >= row) & (col <= row + BLOCK)
    band2 = jnp.concatenate([band] * HEADS_PER_TILE, axis=0)
    first = (lax.broadcasted_iota(jnp.int32, (BLOCK, BLOCK), 1)
             <= lax.broadcasted_iota(jnp.int32, (BLOCK, BLOCK), 0))
    first2 = jnp.concatenate([first] * HEADS_PER_TILE, axis=0)

    def key_slices(n):
        cur = slice(n * BLOCK, (n + 1) * BLOCK)
        if n == 0:
            return (cur,)
        return slice((n - 1) * BLOCK, n * BLOCK), cur

    one_lo = jnp.where(low, 1.0, 0.0).astype(BF16)
    one_hi = jnp.where(low, 0.0, 1.0).astype(BF16)
    for g, (_, r) in enumerate(DSWA_GROUPS):
        v_ref = groups[g][2]
        for z in range(r):
            for n in range(S // r // BLOCK):
                cur = slice(n * BLOCK, (n + 1) * BLOCK)
                vf = v_ref[0, z, 0, cur, :].astype(F32)
                vx_scr[g][0, z, cur, :LANES] = jnp.where(low, vf, 0.0).astype(BF16)
                vx_scr[g][0, z, cur, LANES:] = one_lo
                vx_scr[g][1, z, cur, :LANES] = jnp.where(low, 0.0, vf).astype(BF16)
                vx_scr[g][1, z, cur, LANES:] = one_hi

    def scores(g, z, n):
        q_ref, k_ref, _ = groups[g]
        keys = key_slices(n)
        qf = q_ref[0, z, 0, keys[-1], :].astype(F32)
        kb = jnp.concatenate([k_ref[0, z, 0, sl, :] for sl in keys], axis=0)
        valid = band2 if n > 0 else first2
        qm = jnp.concatenate([jnp.where(low, qf, 0.0), jnp.where(low, 0.0, qf)],
                             axis=0).astype(BF16)
        s = lax.dot_general(qm, kb, (((1,), (1,)), ((), ())), preferred_element_type=F32)
        return jnp.where(valid, s, NEG)

    def finish(s, g, z, n):
        r = DSWA_GROUPS[g][1]
        keys = key_slices(n)
        cur = keys[-1]
        m = jnp.max(s, axis=-1, keepdims=True)
        p = jnp.exp2(s - m).astype(BF16)
        vx = jnp.concatenate([vx_scr[g][hh, z, sl, :] for hh in range(HEADS_PER_TILE)
                              for sl in keys], axis=0)
        ol = jnp.dot(jnp.concatenate([p[:BLOCK], p[BLOCK:]], axis=1), vx,
                     preferred_element_type=F32)
        l = ol[:, LANES:]
        dst = pl.ds(z + r * n * BLOCK, BLOCK, stride=r) if r > 1 else cur
        og_scr[g, dst, :] = ol[:, :LANES] / l
        lse_scr[g, dst, :] = jnp.where(low, m[:BLOCK], m[BLOCK:]) + jnp.log2(l)

    for g, (_, r) in enumerate(DSWA_GROUPS):
        for z in range(r):
            for n in range(S // r // BLOCK):
                finish(scores(g, z, n), g, z, n)

    chunk = 8 * BLOCK

    def combine(i, carry):
        rows = pl.ds(pl.multiple_of(i * chunk, chunk), chunk)
        l0, l1, l2 = lse_scr[0, rows, :], lse_scr[1, rows, :], lse_scr[2, rows, :]
        mx = jnp.maximum(jnp.maximum(l0, l1), l2)
        e0, e1, e2 = jnp.exp2(l0 - mx), jnp.exp2(l1 - mx), jnp.exp2(l2 - mx)
        num = e0 * og_scr[0, rows, :] + e1 * og_scr[1, rows, :] + e2 * og_scr[2, rows, :]
        gate = gate_ref[0, 0, 0, rows, :].astype(F32)
        den = (e0 + e1 + e2) * (1.0 + jnp.exp(-gate))
        o_ref[0, 0, rows, :] = (num * gate / den).astype(BF16)
        return carry

    lax.fori_loop(0, S // chunk, combine, 0)


def _dsw_attn(p0, p1, p2):
    B, _, _, S, _ = p0.shape
    in_specs, args = [], []
    for g, (p, (_, r)) in enumerate(zip((p0, p1, p2), DSWA_GROUPS)):
        for sec in range(3):
            in_specs.append(pl.BlockSpec(
                (1, r, 1, S // r, LANES),
                lambda b, hp, sec=sec: (b, 0, sec * N_TILES + hp, 0, 0)))
            args.append(p)
        if g == 0:
            in_specs.append(pl.BlockSpec(
                (1, 1, 1, S, LANES), lambda b, hp: (b, 0, 3 * N_TILES + hp, 0, 0)))
            args.append(p)
    return pl.pallas_call(
        _dsw_attn_kernel,
        grid=(B, N_TILES),
        in_specs=in_specs,
        out_specs=pl.BlockSpec((1, 1, S, LANES), lambda b, hp: (b, hp, 0, 0)),
        out_shape=jax.ShapeDtypeStruct((B, N_TILES, S, LANES), BF16),
        scratch_shapes=[pltpu.VMEM((len(DSWA_GROUPS), S, LANES), F32),
                        pltpu.VMEM((len(DSWA_GROUPS), S, LANES), F32)] + [
            pltpu.VMEM((HEADS_PER_TILE, r, S // r, 2 * LANES), BF16) for _, r in DSWA_GROUPS],
        compiler_params=pltpu.CompilerParams(
            dimension_semantics=("arbitrary",) * 2, vmem_limit_bytes=VMEM_LIMIT),
        name="dsw_attn",
    )(*args)


def _outproj_kernel(a_ref, w_ref, x_ref, o_ref):
    a = jnp.concatenate([a_ref[0, c] for c in range(N_TILES)], axis=1)
    o_ref[0] = x_ref[0] + jnp.dot(a, w_ref[...], preferred_element_type=F32)


def _outproj(a, w, lj, x):
    B, S, _ = x.shape
    return pl.pallas_call(
        _outproj_kernel,
        grid=(B, S // OUT_ROW_TILE),
        in_specs=[pl.BlockSpec((1, N_TILES, OUT_ROW_TILE, LANES), lambda b, t: (b, 0, t, 0)),
                  pl.BlockSpec((None, INNER, D_MODEL), lambda b, t: (lj, 0, 0)),
                  pl.BlockSpec((1, OUT_ROW_TILE, D_MODEL), lambda b, t: (b, t, 0))],
        out_specs=pl.BlockSpec((1, OUT_ROW_TILE, D_MODEL), lambda b, t: (b, t, 0)),
        out_shape=jax.ShapeDtypeStruct((B, S, D_MODEL), F32),
        compiler_params=pltpu.CompilerParams(
            dimension_semantics=("arbitrary",) * 2, vmem_limit_bytes=VMEM_LIMIT),
        name="outproj",
    )(a, w, x)


def _gain_rows(q_g, k_g, n_plain):
    rows = [jnp.tile(q_g * (SCALE * LOG2E), N_HEADS), jnp.tile(k_g, N_HEADS)]
    rows += [jnp.ones((INNER,), F32)] * n_plain
    return jnp.stack(rows, axis=0).reshape(len(rows), 1, INNER).astype(F32)


def _rotary_tables(positions):
    half = ROT_DIM // 2
    inv_freq = ROPE_THETA ** (-jnp.arange(0, ROT_DIM, 2, dtype=F32) / ROT_DIM)
    ang = positions.astype(F32)[..., None] * inv_freq
    cos, sin = jnp.cos(ang), jnp.sin(ang)
    B, S = positions.shape
    one = jnp.ones((B, S, HEAD_DIM - ROT_DIM), F32)
    zero = jnp.zeros((B, S, HEAD_DIM - ROT_DIM), F32)
    c = jnp.concatenate([cos, cos, one], axis=-1)
    s = jnp.concatenate([-sin, sin, zero], axis=-1)
    return tuple(jnp.tile(t, (1, 1, HEADS_PER_TILE)) for t in (c, s))


def kernel(x, positions, norm_g, fox_w_in, fox_b_f, fox_q_norm, fox_k_norm, fox_w_out,
           dsw_w_in, dsw_q_norm, dsw_k_norm, dsw_w_out):
    B, S, _ = x.shape
    depth = norm_g.shape[0]
    tabs = _rotary_tables(positions)
    fox_w = fox_w_in.astype(BF16)
    fox_wf = jnp.pad(fox_w[:, :, 4 * INNER:], ((0, 0), (0, 0), (0, LANES - N_HEADS)))
    dsw_w = dsw_w_in.astype(BF16)
    fox_wo = fox_w_out.astype(BF16)
    dsw_wo = dsw_w_out.astype(BF16)
    for i in range(depth):
        j = i // 2
        if i % 2 == 0:
            gains = _gain_rows(fox_q_norm[j], fox_k_norm[j], 2)
            qkvg, ft = _inproj(x, norm_g[i], fox_w, j, (0, 1, 2, 3), gains,
                               ("norm", "norm", "plain", "plain"), r=1, w_f=fox_wf)
            c = _fox_c(ft, fox_b_f[j])
            a = _fox_attn(qkvg, c)
            x = _outproj(a, fox_wo, j, x)
        else:
            projs = []
            n_groups = len(DSWA_GROUPS)
            for g, (_, r) in enumerate(DSWA_GROUPS):
                idxs = (g, n_groups + g, 2 * n_groups + g) + ((3 * n_groups,) if g == 0 else ())
                kinds = ("norm", "norm", "plain") + (("plain",) if g == 0 else ())
                gains = _gain_rows(dsw_q_norm[j, g], dsw_k_norm[j, g], len(kinds) - 2)
                projs.append(_inproj(x, norm_g[i], dsw_w, j, idxs, gains, kinds,
                                     r=r, tabs=tabs)[0])
            a = _dsw_attn(*projs)
            x = _outproj(a, dsw_wo, j, x)
    return x
```
